```python
import math
import jax
import jax.numpy as jnp
from jax import lax
import numpy as np

D_MODEL = 1024
BATCH = 8
SEQ = 4096
DEPTH = 2

N_EVEN = (DEPTH + 1) // 2
N_ODD = DEPTH // 2
EPS = 1e-5

SSD_HEADS = 16
SSD_HEAD_DIM = 64
SSD_INNER = SSD_HEADS * SSD_HEAD_DIM
SSD_GROUPS = 4
SSD_STATE = 128
SSD_CONV = 4
SSD_CHUNK = 128
SSD_XBC = SSD_INNER + 2 * SSD_GROUPS * SSD_STATE

CONF_DIM = 1024
CONF_KERNEL = 31

EVEN_IN = SSD_INNER + SSD_XBC + SSD_HEADS + 2 * CONF_DIM
EVEN_MIX = SSD_INNER + CONF_DIM

S5_DIM = 512
S5_GROUP = 16
S5_GROUPS = S5_DIM // S5_GROUP
S5_STATE = 64

RET_HEADS = 4
RET_KEY_DIM = 256
RET_VAL_DIM = 256
RET_CHUNK = 128
RET_QK = RET_HEADS * RET_KEY_DIM
RET_V = RET_HEADS * RET_VAL_DIM
ROPE_BASE = 10000.0

ODD_IN = S5_DIM + 2 * RET_QK + 2 * RET_V
ODD_MIX = S5_DIM + RET_V

FFN_HIDDEN = 256 * math.ceil(8 * D_MODEL / 3 / 256)

kernel_name = 'hybrid_ssd_conformer_s5_retention_block'


def rmsnorm(x, g):
    xf = x.astype(jnp.float32)
    y = xf * lax.rsqrt(jnp.mean(xf * xf, axis=-1, keepdims=True) + EPS)
    return (y * g.astype(jnp.float32)).astype(x.dtype)


def layernorm(x, g, b):
    xf = x.astype(jnp.float32)
    mu = jnp.mean(xf, axis=-1, keepdims=True)
    var = jnp.mean(jnp.square(xf - mu), axis=-1, keepdims=True)
    y = (xf - mu) * lax.rsqrt(var + EPS)
    return (y * g.astype(jnp.float32) + b.astype(jnp.float32)).astype(x.dtype)


def causal_depthwise_conv(x, w, bias):
    k, ch = w.shape
    y = lax.conv_general_dilated(x, w[:, None, :].astype(x.dtype), window_strides=(1,),
                                 padding=[(k - 1, 0)], dimension_numbers=('NWC', 'WIO', 'NWC'),
                                 feature_group_count=ch)
    return y + bias.astype(x.dtype)


def segsum_exp(a):
    t = a.shape[-1]
    cs = jnp.cumsum(a, axis=-1)
    diff = cs[..., :, None] - cs[..., None, :]
    mask = jnp.tril(jnp.ones((t, t), dtype=bool))
    return jnp.exp(jnp.where(mask, diff, -jnp.inf))


def ssd_chunked(xh, dt, a_head, bm, cm):
    b, s, h, p = xh.shape
    g, n = bm.shape[2], bm.shape[3]
    r = h // g
    c, l = s // SSD_CHUNK, SSD_CHUNK
    a = jnp.moveaxis((dt * a_head).reshape(b, c, l, g, r), 2, -1)
    a_cs = jnp.cumsum(a, axis=-1)
    xdt = (xh * dt[..., None]).reshape(b, c, l, g, r, p)
    bc = bm.reshape(b, c, l, g, n)
    cc = cm.reshape(b, c, l, g, n)
    cb = jnp.einsum('bclgn,bcsgn->bcgls', cc, bc)
    scores = cb[:, :, :, None] * segsum_exp(a)
    y_diag = jnp.einsum('bcgrls,bcsgrp->bclgrp', scores, xdt)
    decay_states = jnp.moveaxis(jnp.exp(a_cs[..., -1:] - a_cs), -1, 2)
    states = jnp.einsum('bclgn,bclgrp->bcgrpn', bc, xdt * decay_states[..., None])
    states = jnp.concatenate([jnp.zeros_like(states[:, :1]), states], axis=1)
    chunk_a = jnp.pad(jnp.moveaxis(a_cs[..., -1], 1, -1), [(0, 0), (0, 0), (0, 0), (1, 0)])
    decay_chunk = segsum_exp(chunk_a)
    prev = jnp.einsum('bgrzc,bcgrpn->bzgrpn', decay_chunk, states)[:, :-1]
    state_decay_out = jnp.moveaxis(jnp.exp(a_cs), -1, 2)
    y_off = jnp.einsum('bclgn,bcgrpn->bclgrp', cc, prev) * state_decay_out[..., None]
    return (y_diag + y_off).reshape(b, s, h, p)


def mamba2_mixer(z, xbc, dt_raw, conv_w, conv_b, dt_bias, a_log, d_skip, norm_g):
    b, s, _ = z.shape
    xbc = jax.nn.silu(causal_depthwise_conv(xbc, conv_w, conv_b)).astype(jnp.float32)
    gn = SSD_GROUPS * SSD_STATE
    xs = xbc[..., :SSD_INNER].reshape(b, s, SSD_HEADS, SSD_HEAD_DIM)
    bm = xbc[..., SSD_INNER:SSD_INNER + gn].reshape(b, s, SSD_GROUPS, SSD_STATE)
    cm = xbc[..., SSD_INNER + gn:].reshape(b, s, SSD_GROUPS, SSD_STATE)
    dt = jax.nn.softplus(dt_raw.astype(jnp.float32) + dt_bias.astype(jnp.float32))
    a_head = -jnp.exp(a_log.astype(jnp.float32))
    y = ssd_chunked(xs, dt, a_head, bm, cm) + d_skip.astype(jnp.float32)[:, None] * xs
    y = y.reshape(b, s, SSD_INNER) * jax.nn.silu(z.astype(jnp.float32))
    yg = y.reshape(b, s, SSD_GROUPS, SSD_INNER // SSD_GROUPS)
    yg = yg * lax.rsqrt(jnp.mean(yg * yg, axis=-1, keepdims=True) + EPS)
    return (yg.reshape(b, s, SSD_INNER) * norm_g.astype(jnp.float32)).astype(z.dtype)


def conformer_conv_mixer(u, conv_w, conv_b, ln_g, ln_b):
    a, gate = u[..., :CONF_DIM], u[..., CONF_DIM:]
    h = a * jax.nn.sigmoid(gate)
    h = causal_depthwise_conv(h, conv_w, conv_b)
    h = layernorm(h, ln_g, ln_b)
    return jax.nn.silu(h)


def s5_mixer(u, lam_re, lam_im, b_re, b_im, c_re, c_im, log_dt, d_skip, glu_w, glu_b):
    b, s, _ = u.shape
    f32 = jnp.float32
    uf = u.astype(f32).reshape(b, s, S5_GROUPS, S5_GROUP)
    lam = lax.complex(lam_re.astype(f32), lam_im.astype(f32))
    dt = jnp.exp(log_dt.astype(f32))[:, None]
    lam_bar = jnp.exp(lam * dt)
    b_bar = ((lam_bar - 1.0) / lam)[..., None] * lax.complex(b_re.astype(f32), b_im.astype(f32))
    bu = jnp.einsum('bsgc,gpc->bsgp', uf, b_bar)
    a_el = jnp.broadcast_to(lam_bar, bu.shape)

    def combine(e1, e2):
        a1, x1 = e1
        a2, x2 = e2
        return a2 * a1, a2 * x1 + x2

    _, states = lax.associative_scan(combine, (a_el, bu), axis=1)
    cmat = lax.complex(c_re.astype(f32), c_im.astype(f32))
    y = jnp.einsum('bsgp,gcp->bsgc', states, cmat).real
    y = y + d_skip.astype(f32).reshape(S5_GROUPS, S5_GROUP) * uf
    y = jax.nn.gelu(y.reshape(b, s, S5_DIM))
    y = y * jax.nn.sigmoid(y @ glu_w.astype(f32) + glu_b.astype(f32))
    return y.astype(u.dtype)


def rotary(t, cos, sin):
    half = t.shape[-1] // 2
    t1, t2 = t[..., :half], t[..., half:]
    return jnp.concatenate([t1 * cos - t2 * sin, t1 * sin + t2 * cos], axis=-1)


def retention_mixer(q, k, v, gate, cos, sin, gn_g, gn_b):
    b, s, _ = q.shape
    f32 = jnp.float32
    c, l = s // RET_CHUNK, RET_CHUNK
    q = rotary(q.astype(f32).reshape(b, s, RET_HEADS, RET_KEY_DIM), cos, sin)
    k = rotary(k.astype(f32).reshape(b, s, RET_HEADS, RET_KEY_DIM), cos, sin) * RET_KEY_DIM ** -0.5
    v = v.astype(f32).reshape(b, s, RET_HEADS, RET_VAL_DIM)
    gamma = 1.0 - jnp.exp(jnp.linspace(math.log(1.0 / 32), math.log(1.0 / 512), RET_HEADS, dtype=f32))
    log_g = jnp.log(gamma)
    qc = q.reshape(b, c, l, RET_HEADS, RET_KEY_DIM)
    kc = k.reshape(b, c, l, RET_HEADS, RET_KEY_DIM)
    vc = v.reshape(b, c, l, RET_HEADS, RET_VAL_DIM)
    idx = jnp.arange(l, dtype=f32)
    diff = idx[:, None] - idx[None, :]
    dmat = jnp.where(diff >= 0, jnp.exp(log_g[:, None, None] * jnp.maximum(diff, 0.0)), 0.0)
    scores = jnp.einsum('bclhd,bcshd->bchls', qc, kc) * dmat
    intra = jnp.einsum('bchls,bcshv->bclhv', scores, vc)
    k_dec = jnp.exp(log_g[None, :] * (l - 1.0 - idx)[:, None])
    kv = jnp.einsum('bclhd,bclhv->bchdv', kc * k_dec[:, :, None], vc)
    chunk_g = jnp.exp(log_g * l)[None, :, None, None]

    def step(state, kv_i):
        return chunk_g * state + kv_i, state

    init = jnp.zeros((b, RET_HEADS, RET_KEY_DIM, RET_VAL_DIM), f32)
    _, prev = lax.scan(step, init, jnp.moveaxis(kv, 1, 0))
    prev = jnp.moveaxis(prev, 0, 1)
    q_dec = jnp.exp(log_g[None, :] * (idx + 1.0)[:, None])
    cross = jnp.einsum('bclhd,bchdv->bclhv', qc * q_dec[:, :, None], prev)
    o = (intra + cross).reshape(b, s, RET_HEADS, RET_VAL_DIM)
    mu = jnp.mean(o, axis=-1, keepdims=True)
    var = jnp.mean(jnp.square(o - mu), axis=-1, keepdims=True)
    o = ((o - mu) * lax.rsqrt(var + EPS)).reshape(b, s, RET_V)
    o = o * gn_g.astype(f32) + gn_b.astype(f32)
    return (jax.nn.silu(gate.astype(f32)) * o).astype(gate.dtype)


def even_layer(x, norm_g, w_in, conv_w, conv_b, dt_bias, a_log, d_skip, ssd_norm,
               cf_w, cf_b, cf_g, cf_beta, w_out):
    u = rmsnorm(x, norm_g) @ w_in
    o1 = SSD_INNER
    o2 = o1 + SSD_XBC
    o3 = o2 + SSD_HEADS
    ya = mamba2_mixer(u[..., :o1], u[..., o1:o2], u[..., o2:o3], conv_w, conv_b,
                      dt_bias, a_log, d_skip, ssd_norm)
    yb = conformer_conv_mixer(u[..., o3:], cf_w, cf_b, cf_g, cf_beta)
    return x + jnp.concatenate([ya, yb], axis=-1) @ w_out


def odd_layer(x, cos, sin, norm_g, w_in, lam_re, lam_im, b_re, b_im, c_re, c_im, log_dt,
              s5_d, glu_w, glu_b, gn_g, gn_b, w_out):
    u = rmsnorm(x, norm_g) @ w_in
    o1 = S5_DIM
    o2 = o1 + RET_QK
    o3 = o2 + RET_QK
    o4 = o3 + RET_V
    yc = s5_mixer(u[..., :o1], lam_re, lam_im, b_re, b_im, c_re, c_im, log_dt, s5_d, glu_w, glu_b)
    yd = retention_mixer(u[..., o1:o2], u[..., o2:o3], u[..., o3:o4], u[..., o4:], cos, sin, gn_g, gn_b)
    return x + jnp.concatenate([yc, yd], axis=-1) @ w_out


def swiglu_ffn(x, g, w_gate, w_up, w_down):
    h = rmsnorm(x, g)
    return x + (jax.nn.silu(h @ w_gate) * (h @ w_up)) @ w_down


def _normal(k, shape, scale):
    return scale * jax.random.normal(k, shape, jnp.float32)


def setup_inputs(seed: int = 0) -> dict:
    key = jax.random.key(seed)
    ks = iter(jax.random.split(key, 48))
    E, O, D, F = N_EVEN, N_ODD, D_MODEL, FFN_HIDDEN
    G, P = S5_GROUPS, S5_STATE
    inp = {}
    inp['x'] = _normal(next(ks), (BATCH, SEQ, D), 1.0)
    inp['ev_norm'] = 1.0 + _normal(next(ks), (E, D), 0.02)
    inp['ev_w_in'] = _normal(next(ks), (E, D, EVEN_IN), D ** -0.5)
    inp['ev_conv_w'] = _normal(next(ks), (E, SSD_CONV, SSD_XBC), SSD_CONV ** -0.5)
    inp['ev_conv_b'] = _normal(next(ks), (E, SSD_XBC), 0.02)
    dt0 = jnp.exp(jax.random.uniform(next(ks), (E, SSD_HEADS), jnp.float32, math.log(1e-3), math.log(1e-1)))
    inp['ev_dt_bias'] = dt0 + jnp.log(-jnp.expm1(-dt0))
    inp['ev_a_log'] = jnp.log(jax.random.uniform(next(ks), (E, SSD_HEADS), jnp.float32, 1.0, 16.0))
    inp['ev_d'] = 1.0 + _normal(next(ks), (E, SSD_HEADS), 0.02)
    inp['ev_ssd_norm'] = 1.0 + _normal(next(ks), (E, SSD_INNER), 0.02)
    inp['ev_cf_conv_w'] = _normal(next(ks), (E, CONF_KERNEL, CONF_DIM), CONF_KERNEL ** -0.5)
    inp['ev_cf_conv_b'] = _normal(next(ks), (E, CONF_DIM), 0.02)
    inp['ev_cf_ln_g'] = 1.0 + _normal(next(ks), (E, CONF_DIM), 0.02)
    inp['ev_cf_ln_b'] = _normal(next(ks), (E, CONF_DIM), 0.02)
    inp['ev_w_out'] = _normal(next(ks), (E, EVEN_MIX, D), EVEN_MIX ** -0.5)
    inp['od_norm'] = 1.0 + _normal(next(ks), (O, D), 0.02)
    inp['od_w_in'] = _normal(next(ks), (O, D, ODD_IN), D ** -0.5)
    n_idx = jnp.arange(P, dtype=jnp.float32)
    inp['od_lam_re'] = -0.5 + _normal(next(ks), (O, G, P), 0.01)
    inp['od_lam_im'] = math.pi * n_idx + _normal(next(ks), (O, G, P), 0.01)
    inp['od_b_re'] = _normal(next(ks), (O, G, P, S5_GROUP), (2 * S5_GROUP) ** -0.5)
    inp['od_b_im'] = _normal(next(ks), (O, G, P, S5_GROUP), (2 * S5_GROUP) ** -0.5)
    inp['od_c_re'] = _normal(next(ks), (O, G, S5_GROUP, P), P ** -0.5)
    inp['od_c_im'] = _normal(next(ks), (O, G, S5_GROUP, P), P ** -0.5)
    inp['od_log_dt'] = jax.random.uniform(next(ks), (O, G), jnp.float32, math.log(1e-3), math.log(1e-1))
    inp['od_s5_d'] = _normal(next(ks), (O, S5_DIM), 1.0)
    inp['od_glu_w'] = _normal(next(ks), (O, S5_DIM, S5_DIM), S5_DIM ** -0.5)
    inp['od_glu_b'] = _normal(next(ks), (O, S5_DIM), 0.02)
    inp['od_gn_g'] = 1.0 + _normal(next(ks), (O, RET_V), 0.02)
    inp['od_gn_b'] = _normal(next(ks), (O, RET_V), 0.02)
    inp['od_w_out'] = _normal(next(ks), (O, ODD_MIX, D), ODD_MIX ** -0.5)
    inp['ffn_norm'] = 1.0 + _normal(next(ks), (DEPTH, D), 0.02)
    inp['ffn_w_gate'] = _normal(next(ks), (DEPTH, D, F), D ** -0.5)
    inp['ffn_w_up'] = _normal(next(ks), (DEPTH, D, F), D ** -0.5)
    inp['ffn_w_down'] = _normal(next(ks), (DEPTH, F, D), F ** -0.5)
    inp['final_norm'] = 1.0 + _normal(next(ks), (D,), 0.02)
    return inp


def reference(x, ev_norm, ev_w_in, ev_conv_w, ev_conv_b, ev_dt_bias, ev_a_log, ev_d, ev_ssd_norm,
              ev_cf_conv_w, ev_cf_conv_b, ev_cf_ln_g, ev_cf_ln_b, ev_w_out,
              od_norm, od_w_in, od_lam_re, od_lam_im, od_b_re, od_b_im, od_c_re, od_c_im, od_log_dt,
              od_s5_d, od_glu_w, od_glu_b, od_gn_g, od_gn_b, od_w_out,
              ffn_norm, ffn_w_gate, ffn_w_up, ffn_w_down, final_norm):
    s = x.shape[1]
    inv_freq = ROPE_BASE ** (-jnp.arange(0, RET_KEY_DIM, 2, dtype=jnp.float32) / RET_KEY_DIM)
    ang = jnp.arange(s, dtype=jnp.float32)[:, None] * inv_freq[None, :]
    cos = jnp.cos(ang)[:, None, :]
    sin = jnp.sin(ang)[:, None, :]
    for layer in range(DEPTH):
        i = layer // 2
        if layer % 2 == 0:
            x = even_layer(x, ev_norm[i], ev_w_in[i], ev_conv_w[i], ev_conv_b[i], ev_dt_bias[i],
                           ev_a_log[i], ev_d[i], ev_ssd_norm[i], ev_cf_conv_w[i], ev_cf_conv_b[i],
                           ev_cf_ln_g[i], ev_cf_ln_b[i], ev_w_out[i])
        else:
            x = odd_layer(x, cos, sin, od_norm[i], od_w_in[i], od_lam_re[i], od_lam_im[i],
                          od_b_re[i], od_b_im[i], od_c_re[i], od_c_im[i], od_log_dt[i], od_s5_d[i],
                          od_glu_w[i], od_glu_b[i], od_gn_g[i], od_gn_b[i], od_w_out[i])
        x = swiglu_ffn(x, ffn_norm[layer], ffn_w_gate[layer], ffn_w_up[layer], ffn_w_down[layer])
    return rmsnorm(x, final_norm)
```

```python
import functools
import math

import numpy as np
import jax
import jax.numpy as jnp
from jax import lax
from jax.experimental import pallas as pl
from jax.experimental.pallas import tpu as pltpu

F32 = jnp.float32
BF16 = jnp.bfloat16
ACT_DTYPE = BF16

D_MODEL = 1024
EPS = 1e-5

SSD_HEADS = 16
SSD_HEAD_DIM = 64
SSD_INNER = SSD_HEADS * SSD_HEAD_DIM
SSD_GROUPS = 4
SSD_STATE = 128
SSD_CONV = 4
SSD_CHUNK = 128
SSD_XBC = SSD_INNER + 2 * SSD_GROUPS * SSD_STATE

CONF_DIM = 1024
CONF_KERNEL = 31

S5_DIM = 512
S5_GROUP = 16
S5_GROUPS = S5_DIM // S5_GROUP
S5_STATE = 64
S5_LANES = S5_GROUPS * S5_STATE
S5_HALF = S5_LANES // 2

RET_HEADS = 4
RET_KEY_DIM = 256
RET_VAL_DIM = 256
RET_CHUNK = 128
RET_QK = RET_HEADS * RET_KEY_DIM
RET_V = RET_HEADS * RET_VAL_DIM
ROPE_BASE = 10000.0

LANES = 128
SUBLANES = 8
VMEM_LIMIT_BYTES = 56 * 1024 * 1024

PROJ_ROWS = 1024
FFN_COLS = 256
CONF_ROWS = 256
CONF_HALO = 32
S5_STEPS = 64


def _cparams(*sem):
    return pltpu.CompilerParams(dimension_semantics=sem, vmem_limit_bytes=VMEM_LIMIT_BYTES)


def _dot(a, b):
    return jnp.dot(a, b, preferred_element_type=F32)


def _dot_nt(a, b):
    return lax.dot_general(a, b, (((1,), (1,)), ((), ())), preferred_element_type=F32)


def _split3(x):
    x1 = x.astype(BF16)
    r1 = x - x1.astype(F32)
    x2 = r1.astype(BF16)
    x3 = (r1 - x2.astype(F32)).astype(BF16)
    return x1, x2, x3


def _rms(x, g):
    return x * lax.rsqrt(jnp.mean(x * x, axis=-1, keepdims=True) + EPS) * g


def _norm_proj_kernel(*refs, has_aux):
    if has_aux:
        x_ref, g_ref, w_ref, waux_ref, o_ref, oaux_ref, h_ref = refs
    else:
        x_ref, g_ref, w_ref, o_ref, h_ref = refs

    @pl.when(pl.program_id(1) == 0)
    def _():
        h = _rms(x_ref[...], g_ref[...]).astype(BF16)
        h_ref[...] = h
        if has_aux:
            oaux_ref[...] = _dot(h, waux_ref[...])

    o_ref[...] = _dot(h_ref[...], w_ref[...]).astype(o_ref.dtype)


def _norm_proj(x, g, w, w_aux, *, tn, name):
    m, d = x.shape
    n = w.shape[1]
    tm = PROJ_ROWS
    has_aux = w_aux is not None
    in_specs = [pl.BlockSpec((tm, d), lambda i, j: (i, 0)),
                pl.BlockSpec((1, d), lambda i, j: (0, 0)),
                pl.BlockSpec((d, tn), lambda i, j: (0, j))]
    out_specs = [pl.BlockSpec((tm, tn), lambda i, j: (i, j))]
    out_shape = [jax.ShapeDtypeStruct((m, n), ACT_DTYPE)]
    args = [x, g.reshape(1, d), w]
    if has_aux:
        in_specs.append(pl.BlockSpec((d, LANES), lambda i, j: (0, 0)))
        out_specs.append(pl.BlockSpec((tm, LANES), lambda i, j: (i, 0)))
        out_shape.append(jax.ShapeDtypeStruct((m, LANES), F32))
        args.append(w_aux)
    return pl.pallas_call(
        functools.partial(_norm_proj_kernel, has_aux=has_aux),
        grid=(m // tm, n // tn),
        in_specs=in_specs, out_specs=out_specs, out_shape=out_shape,
        scratch_shapes=[pltpu.VMEM((tm, d), BF16)],
        compiler_params=_cparams("parallel", "arbitrary"),
        name=name,
    )(*args)


def _pair_lanes(v, j, low):
    return jnp.where(low, v[:, 2 * j:2 * j + 1], v[:, 2 * j + 1:2 * j + 2])


def _ssd_kernel(xbc_ref, z_ref, dt_ref, cw_ref, cb_ref, dtb_ref, alog_ref, dskip_ref, ng_ref,
                o_ref, ext_ref, state_ref):
    L = SSD_CHUNK
    N = SSD_STATE

    @pl.when(pl.program_id(1) == 0)
    def _():
        ext_ref[0:SUBLANES, :] = jnp.zeros((SUBLANES, SSD_XBC), F32)
        state_ref[...] = jnp.zeros(state_ref.shape, F32)

    ext_ref[SUBLANES:SUBLANES + L, :] = xbc_ref[...].astype(F32)
    acc = cb_ref[...]
    for j in range(SSD_CONV):
        acc = acc + cw_ref[j:j + 1, :] * ext_ref[pl.ds(SUBLANES - (SSD_CONV - 1) + j, L), :]
    ext_ref[0:SUBLANES, :] = ext_ref[L:L + SUBLANES, :]
    xbc = acc * jax.nn.sigmoid(acc)
    xs = xbc[:, :SSD_INNER]
    bm = xbc[:, SSD_INNER:SSD_INNER + SSD_GROUPS * N]
    cm = xbc[:, SSD_INNER + SSD_GROUPS * N:]

    lane = lax.broadcasted_iota(jnp.int32, (L, LANES), 1)
    dt = jax.nn.softplus(dt_ref[...] + dtb_ref[...])
    a = jnp.where(lane < SSD_HEADS, dt * (-jnp.exp(alog_ref[...])), 0.0)
    row = lax.broadcasted_iota(jnp.int32, (L, L), 0)
    col = lax.broadcasted_iota(jnp.int32, (L, L), 1)
    tril = row >= col
    tri = tril.astype(BF16)
    a1, a2, a3 = _split3(a)
    a_cs = _dot(tri, a1) + _dot(tri, a2) + _dot(tri, a3)
    a_cs_t = a_cs.T
    ea = jnp.exp(a_cs)
    a_last = a_cs[L - 1:L, :]
    ds = jnp.exp(a_last - a_cs)
    ea_last = jnp.exp(a_last)

    low = lane < SSD_HEAD_DIM
    y_tiles = []
    for g in range(SSD_GROUPS):
        bg = bm[:, g * N:(g + 1) * N]
        cg = cm[:, g * N:(g + 1) * N].astype(BF16)
        bg_t = bg.T.astype(BF16)
        cb = _dot(cg, bg_t)
        prev = state_ref[g]
        y_off = _dot(cg, prev.astype(BF16))
        xd_tiles = []
        ea_last_tiles = []
        for jj in range(2):
            j = 2 * g + jj
            dt_p = _pair_lanes(dt, j, low)
            ea_p = _pair_lanes(ea, j, low)
            ds_p = _pair_lanes(ds, j, low)
            xs_p = xs[:, j * LANES:(j + 1) * LANES]
            xdt_p = xs_p * dt_p
            scores = []
            for h in (2 * j, 2 * j + 1):
                diff = a_cs[:, h:h + 1] - a_cs_t[h:h + 1, :]
                lmat = jnp.exp(jnp.where(tril, diff, -jnp.inf))
                scores.append((cb * lmat).astype(BF16))
            lhs = jnp.concatenate(scores, axis=1)
            rhs = jnp.concatenate([jnp.where(low, xdt_p, 0.0), jnp.where(low, 0.0, xdt_p)],
                                  axis=0).astype(BF16)
            y_diag = _dot(lhs, rhs)
            y_p = (y_diag + y_off[:, jj * LANES:(jj + 1) * LANES] * ea_p
                   + dskip_ref[:, j * LANES:(j + 1) * LANES] * xs_p)
            y_tiles.append(y_p)
            xd_tiles.append((xdt_p * ds_p).astype(BF16))
            ea_last_tiles.append(jnp.where(low[0:1, :], ea_last[:, 2 * j:2 * j + 1], ea_last[:, 2 * j + 1:2 * j + 2]))
        xd = jnp.concatenate(xd_tiles, axis=1)
        decay = jnp.concatenate(ea_last_tiles, axis=1)
        state_ref[g] = prev * decay + _dot(bg_t, xd)

    y = jnp.concatenate(y_tiles, axis=1)
    z = z_ref[...].astype(F32)
    y = y * (z * jax.nn.sigmoid(z))
    gw = SSD_INNER // SSD_GROUPS
    outs = []
    for g in range(SSD_GROUPS):
        yg = y[:, g * gw:(g + 1) * gw]
        outs.append(yg * lax.rsqrt(jnp.mean(yg * yg, axis=-1, keepdims=True) + EPS))
    o_ref[...] = (jnp.concatenate(outs, axis=1) * ng_ref[...]).astype(o_ref.dtype)


def _ssd_mixer(u_wide, dt_raw, conv_w, conv_b, dt_bias, a_log, d_skip, norm_g, batch, seq):
    L = SSD_CHUNK
    nc = seq // L
    m = batch * seq

    def pad_heads(v):
        return jnp.pad(v.astype(F32), (0, LANES - SSD_HEADS)).reshape(1, LANES)

    dskip = jnp.repeat(d_skip.astype(F32), SSD_HEAD_DIM).reshape(1, SSD_INNER)
    return pl.pallas_call(
        _ssd_kernel,
        grid=(batch, nc),
        in_specs=[
            pl.BlockSpec((L, SSD_XBC), lambda b, c: (b * nc + c, 0)),
            pl.BlockSpec((L, SSD_INNER), lambda b, c: (b * nc + c, 2 * SSD_XBC // SSD_INNER)),
            pl.BlockSpec((L, LANES), lambda b, c: (b * nc + c, 0)),
            pl.BlockSpec((SSD_CONV, SSD_XBC), lambda b, c: (0, 0)),
            pl.BlockSpec((1, SSD_XBC), lambda b, c: (0, 0)),
            pl.BlockSpec((1, LANES), lambda b, c: (0, 0)),
            pl.BlockSpec((1, LANES), lambda b, c: (0, 0)),
            pl.BlockSpec((1, SSD_INNER), lambda b, c: (0, 0)),
            pl.BlockSpec((1, SSD_INNER), lambda b, c: (0, 0)),
        ],
        out_specs=pl.BlockSpec((L, SSD_INNER), lambda b, c: (b * nc + c, 0)),
        out_shape=jax.ShapeDtypeStruct((m, SSD_INNER), ACT_DTYPE),
        scratch_shapes=[pltpu.VMEM((L + 2 * SUBLANES, SSD_XBC), F32),
                        pltpu.VMEM((SSD_GROUPS, SSD_STATE, SSD_INNER // SSD_GROUPS), F32)],
        compiler_params=_cparams("parallel", "arbitrary"),
        name="ssd_mixer",
    )(u_wide, u_wide, dt_raw, conv_w.astype(F32), conv_b.astype(F32).reshape(1, SSD_XBC),
      pad_heads(dt_bias), pad_heads(a_log), dskip, norm_g.astype(F32).reshape(1, SSD_INNER))


def _conf_kernel(ag_ref, w_ref, b_ref, g_ref, beta_ref, o_ref, ext_ref, y_ref):
    T = CONF_ROWS
    H = CONF_HALO

    @pl.when(pl.program_id(1) == 0)
    def _():
        ext_ref[0:H, :] = jnp.zeros((H, CONF_DIM), F32)

    a = ag_ref[:, :CONF_DIM].astype(F32)
    gate = ag_ref[:, CONF_DIM:].astype(F32)
    ext_ref[H:H + T, :] = a * jax.nn.sigmoid(gate)

    rows, cols = 64, LANES
    base = H - (CONF_KERNEL - 1)
    for r in range(T // rows):
        for c in range(CONF_DIM // cols):
            acc = jnp.broadcast_to(b_ref[:, c * cols:(c + 1) * cols], (rows, cols))
            for j in range(CONF_KERNEL):
                acc = acc + (w_ref[j:j + 1, c * cols:(c + 1) * cols]
                             * ext_ref[pl.ds(base + j + r * rows, rows), c * cols:(c + 1) * cols])
            y_ref[r * rows:(r + 1) * rows, c * cols:(c + 1) * cols] = acc
    ext_ref[0:H, :] = ext_ref[T:T + H, :]

    h = y_ref[...]
    mu = jnp.mean(h, axis=-1, keepdims=True)
    var = jnp.mean(jnp.square(h - mu), axis=-1, keepdims=True)
    hn = (h - mu) * lax.rsqrt(var + EPS) * g_ref[...] + beta_ref[...]
    o_ref[...] = (hn * jax.nn.sigmoid(hn)).astype(o_ref.dtype)


def _conf_mixer(u_wide, conv_w, conv_b, ln_g, ln_b, batch, seq):
    T = CONF_ROWS
    nt = seq // T
    m = batch * seq
    vec = lambda v: v.astype(F32).reshape(1, CONF_DIM)
    return pl.pallas_call(
        _conf_kernel,
        grid=(batch, nt),
        in_specs=[
            pl.BlockSpec((T, 2 * CONF_DIM), lambda b, c: (b * nt + c, 1)),
            pl.BlockSpec((CONF_KERNEL, CONF_DIM), lambda b, c: (0, 0)),
            pl.BlockSpec((1, CONF_DIM), lambda b, c: (0, 0)),
            pl.BlockSpec((1, CONF_DIM), lambda b, c: (0, 0)),
            pl.BlockSpec((1, CONF_DIM), lambda b, c: (0, 0)),
        ],
        out_specs=pl.BlockSpec((T, CONF_DIM), lambda b, c: (b * nt + c, 0)),
        out_shape=jax.ShapeDtypeStruct((m, CONF_DIM), ACT_DTYPE),
        scratch_shapes=[pltpu.VMEM((T + CONF_HALO, CONF_DIM), F32),
                        pltpu.VMEM((T, CONF_DIM), F32)],
        compiler_params=_cparams("parallel", "arbitrary"),
        name="conformer_mixer",
    )(u_wide, conv_w.astype(F32), vec(conv_b), vec(ln_g), vec(ln_b))


def _out_proj_kernel(x_ref, a_ref, b_ref, wa_ref, wb_ref, o_ref):
    o_ref[...] = x_ref[...] + _dot(a_ref[...], wa_ref[...]) + _dot(b_ref[...], wb_ref[...])


def _out_proj(x, a, b, w, name):
    m, d = x.shape
    ka, kb = a.shape[1], b.shape[1]
    tm = PROJ_ROWS
    wa = w[:ka].astype(BF16)
    wb = w[ka:].astype(BF16)
    return pl.pallas_call(
        _out_proj_kernel,
        grid=(m // tm,),
        in_specs=[pl.BlockSpec((tm, d), lambda i: (i, 0)),
                  pl.BlockSpec((tm, ka), lambda i: (i, 0)),
                  pl.BlockSpec((tm, kb), lambda i: (i, 0)),
                  pl.BlockSpec((ka, d), lambda i: (0, 0)),
                  pl.BlockSpec((kb, d), lambda i: (0, 0))],
        out_specs=pl.BlockSpec((tm, d), lambda i: (i, 0)),
        out_shape=jax.ShapeDtypeStruct((m, d), F32),
        compiler_params=_cparams("parallel"),
        name=name,
    )(x, a, b, wa, wb)


def _ffn_kernel(x_ref, g_ref, wg_ref, wu_ref, wd_ref, fg_ref, o_ref, h_ref, acc_ref, *, final_norm):
    j = pl.program_id(1)

    @pl.when(j == 0)
    def _():
        h_ref[...] = _rms(x_ref[...], g_ref[...]).astype(BF16)

    h = h_ref[...]
    gate = _dot(h, wg_ref[...])
    up = _dot(h, wu_ref[...])
    act = (gate * jax.nn.sigmoid(gate) * up).astype(BF16)
    part = _dot(act, wd_ref[...])

    @pl.when(j == 0)
    def _():
        acc_ref[...] = part

    @pl.when(j > 0)
    def _():
        acc_ref[...] += part

    @pl.when(j == pl.num_programs(1) - 1)
    def _():
        y = x_ref[...] + acc_ref[...]
        if final_norm:
            y = _rms(y, fg_ref[...])
        o_ref[...] = y


def _ffn(x, g, w_gate, w_up, w_down, final_g, name):
    m, d = x.shape
    f = w_gate.shape[1]
    tm, tf = PROJ_ROWS, FFN_COLS
    final_norm = final_g is not None
    fg = (final_g if final_norm else jnp.ones((d,), F32)).astype(F32).reshape(1, d)
    return pl.pallas_call(
        functools.partial(_ffn_kernel, final_norm=final_norm),
        grid=(m // tm, f // tf),
        in_specs=[pl.BlockSpec((tm, d), lambda i, j: (i, 0)),
                  pl.BlockSpec((1, d), lambda i, j: (0, 0)),
                  pl.BlockSpec((d, tf), lambda i, j: (0, j)),
                  pl.BlockSpec((d, tf), lambda i, j: (0, j)),
                  pl.BlockSpec((tf, d), lambda i, j: (j, 0)),
                  pl.BlockSpec((1, d), lambda i, j: (0, 0))],
        out_specs=pl.BlockSpec((tm, d), lambda i, j: (i, 0)),
        out_shape=jax.ShapeDtypeStruct((m, d), F32),
        scratch_shapes=[pltpu.VMEM((tm, d), BF16), pltpu.VMEM((tm, d), F32)],
        compiler_params=_cparams("parallel", "arbitrary"),
        name=name,
    )(x, g.astype(F32).reshape(1, d), w_gate.astype(BF16), w_up.astype(BF16), w_down.astype(BF16), fg)


def _s5_kernel(u_ref, perm_ref, permt_ref, bre_ref, bim_ref, cre_ref, cim_ref, lre_ref, lim_ref, d_ref,
               gw_ref, gb_ref, o_ref, sre_ref, sim_ref, st_ref):
    nb, ts, dim = u_ref.shape
    rows = nb * ts
    half_in = dim // 2

    @pl.when(pl.program_id(0) == 0)
    def _():
        st_ref[...] = jnp.zeros(st_ref.shape, F32)

    perm = perm_ref[...]
    u_bm = u_ref[...].reshape(rows, dim)
    u_tm = _dot(perm, u_bm)
    u16 = u_tm.astype(BF16)

    for hf in range(2):
        uh = u16[:, hf * half_in:(hf + 1) * half_in]
        sre_ref[:, hf * S5_HALF:(hf + 1) * S5_HALF] = _dot(uh, bre_ref[hf])
        sim_ref[:, hf * S5_HALF:(hf + 1) * S5_HALF] = _dot(uh, bim_ref[hf])

    for hf in range(2):
        sl = slice(hf * S5_HALF, (hf + 1) * S5_HALF)
        lr = lre_ref[:, sl]
        li = lim_ref[:, sl]

        def step(t, carry, sl=sl, lr=lr, li=li):
            sr, si = carry
            r0 = pl.multiple_of(t * nb, nb)
            nr = lr * sr - li * si + sre_ref[pl.ds(r0, nb), sl]
            ni = lr * si + li * sr + sim_ref[pl.ds(r0, nb), sl]
            sre_ref[pl.ds(r0, nb), sl] = nr
            sim_ref[pl.ds(r0, nb), sl] = ni
            return nr, ni

        sr, si = lax.fori_loop(0, ts, step, (st_ref[0, :, sl], st_ref[1, :, sl]), unroll=2)
        st_ref[0, :, sl] = sr
        st_ref[1, :, sl] = si

    ys = []
    for hf in range(2):
        sl = slice(hf * S5_HALF, (hf + 1) * S5_HALF)
        ys.append(_dot(sre_ref[:, sl].astype(BF16), cre_ref[hf]) + _dot(sim_ref[:, sl].astype(BF16), cim_ref[hf]))
    y = jnp.concatenate(ys, axis=1) + d_ref[...] * u_tm
    y = jax.nn.gelu(y)
    y = y * jax.nn.sigmoid(_dot(y.astype(BF16), gw_ref[...]) + gb_ref[...])
    y_bm = _dot(permt_ref[...], y.astype(BF16))
    o_ref[...] = y_bm.reshape(nb, ts, dim).astype(o_ref.dtype)


def _s5_mixer(u_wide3, lam_re, lam_im, b_re, b_im, c_re, c_im, log_dt, d_skip, glu_w, glu_b, batch, seq):
    ts = S5_STEPS
    rows = batch * ts
    G, P, C = S5_GROUPS, S5_STATE, S5_GROUP
    hg = G // 2
    lam = lax.complex(lam_re.astype(F32), lam_im.astype(F32))
    dt = jnp.exp(log_dt.astype(F32))[:, None]
    lam_bar = jnp.exp(lam * dt)
    b_bar = ((lam_bar - 1.0) / lam)[..., None] * lax.complex(b_re.astype(F32), b_im.astype(F32))
    eye = jnp.eye(hg, dtype=F32)

    def blockdiag_in(v):
        blk = v.transpose(0, 2, 1).reshape(2, hg, C, P)
        return jnp.einsum('hgcp,gk->hgckp', blk, eye).reshape(2, hg * C, hg * P).astype(BF16)

    def blockdiag_out(v):
        blk = v.transpose(0, 2, 1).reshape(2, hg, P, C)
        return jnp.einsum('hgpc,gk->hgpkc', blk, eye).reshape(2, hg * P, hg * C).astype(BF16)

    bre = blockdiag_in(jnp.real(b_bar))
    bim = blockdiag_in(jnp.imag(b_bar))
    cre = blockdiag_out(c_re.astype(F32))
    cim = blockdiag_out(-c_im.astype(F32))
    lre = jnp.broadcast_to(jnp.real(lam_bar).reshape(1, G * P), (batch, G * P))
    lim = jnp.broadcast_to(jnp.imag(lam_bar).reshape(1, G * P), (batch, G * P))
    r = np.arange(rows)
    perm_np = np.zeros((rows, rows), np.float32)
    perm_np[r, (r % batch) * ts + r // batch] = 1.0
    perm = jnp.asarray(perm_np, dtype=BF16)
    perm_t = jnp.asarray(perm_np.T, dtype=BF16)
    ublk = u_wide3.shape[2] // S5_DIM - 1
    const2 = lambda i: (0, 0)
    const3 = lambda i: (0, 0, 0)
    return pl.pallas_call(
        _s5_kernel,
        grid=(seq // ts,),
        in_specs=[
            pl.BlockSpec((batch, ts, S5_DIM), lambda i: (0, i, ublk)),
            pl.BlockSpec((rows, rows), const2),
            pl.BlockSpec((rows, rows), const2),
            pl.BlockSpec((2, hg * C, hg * P), const3),
            pl.BlockSpec((2, hg * C, hg * P), const3),
            pl.BlockSpec((2, hg * P, hg * C), const3),
            pl.BlockSpec((2, hg * P, hg * C), const3),
            pl.BlockSpec((batch, G * P), const2),
            pl.BlockSpec((batch, G * P), const2),
            pl.BlockSpec((1, S5_DIM), const2),
            pl.BlockSpec((S5_DIM, S5_DIM), const2),
            pl.BlockSpec((1, S5_DIM), const2),
        ],
        out_specs=pl.BlockSpec((batch, ts, S5_DIM), lambda i: (0, i, 0)),
        out_shape=jax.ShapeDtypeStruct((batch, seq, S5_DIM), ACT_DTYPE),
        scratch_shapes=[pltpu.VMEM((rows, S5_LANES), F32), pltpu.VMEM((rows, S5_LANES), F32),
                        pltpu.VMEM((2, batch, S5_LANES), F32)],
        compiler_params=_cparams("arbitrary"),
        name="s5_mixer",
    )(u_wide3, perm, perm_t, bre, bim, cre, cim, lre, lim, d_skip.astype(F32).reshape(1, S5_DIM),
      glu_w.astype(BF16), glu_b.astype(F32).reshape(1, S5_DIM))


def _ret_kernel(q_ref, k_ref, v_ref, gate_ref, cos_ref, sin_ref, dmat_ref, qdec_ref, kdec_ref, cg_ref,
                gng_ref, gnb_ref, o_ref, state_ref):
    L = RET_CHUNK
    dk, dv = RET_KEY_DIM, RET_VAL_DIM
    half = dk // 2

    @pl.when(pl.program_id(1) == 0)
    def _():
        state_ref[...] = jnp.zeros(state_ref.shape, F32)

    cos = cos_ref[...]
    sin = sin_ref[...]

    def rot(t):
        t1, t2 = t[:, :half], t[:, half:]
        return jnp.concatenate([t1 * cos - t2 * sin, t1 * sin + t2 * cos], axis=1)

    for h in range(RET_HEADS):
        q = rot(q_ref[:, h * dk:(h + 1) * dk].astype(F32))
        k = rot(k_ref[:, h * dk:(h + 1) * dk].astype(F32)) * (dk ** -0.5)
        v = v_ref[:, h * dv:(h + 1) * dv]
        q16 = q.astype(BF16)
        k_t = k.T
        scores = _dot(q16, k_t.astype(BF16)) * dmat_ref[h]
        prev = state_ref[h]
        o = _dot(scores.astype(BF16), v) + _dot(q16, prev.astype(BF16)) * qdec_ref[:, h:h + 1]
        kv = _dot((k_t * kdec_ref[h:h + 1, :]).astype(BF16), v)
        state_ref[h] = prev * cg_ref[h:h + 1, :] + kv
        mu = jnp.mean(o, axis=-1, keepdims=True)
        var = jnp.mean(jnp.square(o - mu), axis=-1, keepdims=True)
        on = (o - mu) * lax.rsqrt(var + EPS) * gng_ref[:, h * dv:(h + 1) * dv] + gnb_ref[:, h * dv:(h + 1) * dv]
        gt = gate_ref[:, h * dv:(h + 1) * dv].astype(F32)
        o_ref[:, h * dv:(h + 1) * dv] = (gt * jax.nn.sigmoid(gt) * on).astype(o_ref.dtype)


def _ret_mixer(u_wide, gn_g, gn_b, batch, seq):
    L = RET_CHUNK
    nc = seq // L
    m = batch * seq
    inv_freq = ROPE_BASE ** (-jnp.arange(0, RET_KEY_DIM, 2, dtype=F32) / RET_KEY_DIM)
    ang = jnp.arange(seq, dtype=F32)[:, None] * inv_freq[None, :]
    cos, sin = jnp.cos(ang), jnp.sin(ang)
    gamma = 1.0 - jnp.exp(jnp.linspace(math.log(1.0 / 32), math.log(1.0 / 512), RET_HEADS, dtype=F32))
    log_g = jnp.log(gamma)
    idx = jnp.arange(L, dtype=F32)
    diff = idx[:, None] - idx[None, :]
    dmat = jnp.where(diff >= 0, jnp.exp(log_g[:, None, None] * jnp.maximum(diff, 0.0)), 0.0)
    kdec = jnp.exp(log_g[:, None] * (L - 1.0 - idx)[None, :])
    kdec = jnp.pad(kdec, ((0, SUBLANES - RET_HEADS), (0, 0)))
    qdec = jnp.exp(log_g[None, :] * (idx + 1.0)[:, None])
    qdec = jnp.pad(qdec, ((0, 0), (0, LANES - RET_HEADS)))
    cg = jnp.broadcast_to(jnp.exp(log_g * L)[:, None], (RET_HEADS, RET_VAL_DIM))
    cg = jnp.pad(cg, ((0, SUBLANES - RET_HEADS), (0, 0)))
    row = lambda b, c: b * nc + c
    return pl.pallas_call(
        _ret_kernel,
        grid=(batch, nc),
        in_specs=[
            pl.BlockSpec((L, RET_QK), lambda b, c: (row(b, c), 0)),
            pl.BlockSpec((L, RET_QK), lambda b, c: (row(b, c), 1)),
            pl.BlockSpec((L, RET_V), lambda b, c: (row(b, c), 2)),
            pl.BlockSpec((L, RET_V), lambda b, c: (row(b, c), 3)),
            pl.BlockSpec((L, RET_KEY_DIM // 2), lambda b, c: (c, 0)),
            pl.BlockSpec((L, RET_KEY_DIM // 2), lambda b, c: (c, 0)),
            pl.BlockSpec((RET_HEADS, L, L), lambda b, c: (0, 0, 0)),
            pl.BlockSpec((L, LANES), lambda b, c: (0, 0)),
            pl.BlockSpec((SUBLANES, L), lambda b, c: (0, 0)),
            pl.BlockSpec((SUBLANES, RET_VAL_DIM), lambda b, c: (0, 0)),
            pl.BlockSpec((1, RET_V), lambda b, c: (0, 0)),
            pl.BlockSpec((1, RET_V), lambda b, c: (0, 0)),
        ],
        out_specs=pl.BlockSpec((L, RET_V), lambda b, c: (row(b, c), 0)),
        out_shape=jax.ShapeDtypeStruct((m, RET_V), ACT_DTYPE),
        scratch_shapes=[pltpu.VMEM((RET_HEADS, RET_KEY_DIM, RET_VAL_DIM), F32)],
        compiler_params=_cparams("parallel", "arbitrary"),
        name="retention_mixer",
    )(u_wide, u_wide, u_wide, u_wide, cos, sin, dmat, qdec, kdec, cg,
      gn_g.astype(F32).reshape(1, RET_V), gn_b.astype(F32).reshape(1, RET_V))


def kernel(x, ev_norm, ev_w_in, ev_conv_w, ev_conv_b, ev_dt_bias, ev_a_log, ev_d, ev_ssd_norm, ev_cf_conv_w, ev_cf_conv_b, ev_cf_ln_g, ev_cf_ln_b, ev_w_out, od_norm, od_w_in, od_lam_re, od_lam_im, od_b_re, od_b_im, od_c_re, od_c_im, od_log_dt, od_s5_d, od_glu_w, od_glu_b, od_gn_g, od_gn_b, od_w_out, ffn_norm, ffn_w_gate, ffn_w_up, ffn_w_down, final_norm):
    batch, seq, d = x.shape
    m = batch * seq
    xf = x.reshape(m, d)

    w = ev_w_in[0]
    o1 = SSD_INNER
    o2 = o1 + SSD_XBC
    o3 = o2 + SSD_HEADS
    w_main = jnp.concatenate([w[:, o1:o2], w[:, o3:], w[:, :o1]], axis=1).astype(BF16)
    w_dt = jnp.pad(w[:, o2:o3], ((0, 0), (0, LANES - SSD_HEADS))).astype(BF16)
    u_wide, dt_raw = _norm_proj(xf, ev_norm[0], w_main, w_dt, tn=1024, name="even_in_proj")
    ya = _ssd_mixer(u_wide, dt_raw, ev_conv_w[0], ev_conv_b[0], ev_dt_bias[0], ev_a_log[0], ev_d[0],
                    ev_ssd_norm[0], batch, seq)
    yb = _conf_mixer(u_wide, ev_cf_conv_w[0], ev_cf_conv_b[0], ev_cf_ln_g[0], ev_cf_ln_b[0], batch, seq)
    xf = _out_proj(xf, ya, yb, ev_w_out[0], "even_out_proj")
    xf = _ffn(xf, ffn_norm[0], ffn_w_gate[0], ffn_w_up[0], ffn_w_down[0], None, "ffn0")

    w = od_w_in[0]
    w_main = jnp.concatenate([w[:, S5_DIM:], w[:, :S5_DIM]], axis=1).astype(BF16)
    (u_wide,) = _norm_proj(xf, od_norm[0], w_main, None, tn=1536, name="odd_in_proj")
    yc = _s5_mixer(u_wide.reshape(batch, seq, -1), od_lam_re[0], od_lam_im[0], od_b_re[0], od_b_im[0],
                   od_c_re[0], od_c_im[0], od_log_dt[0], od_s5_d[0], od_glu_w[0], od_glu_b[0], batch, seq)
    yd = _ret_mixer(u_wide, od_gn_g[0], od_gn_b[0], batch, seq)
    xf = _out_proj(xf, yc.reshape(m, S5_DIM), yd, od_w_out[0], "odd_out_proj")
    xf = _ffn(xf, ffn_norm[1], ffn_w_gate[1], ffn_w_up[1], ffn_w_down[1], final_norm, "ffn1")
    return xf.reshape(batch, seq, d)
```

```python
import functools
import math

import numpy as np
import jax
import jax.numpy as jnp
from jax import lax
from jax.experimental import pallas as pl
from jax.experimental.pallas import tpu as pltpu

F32 = jnp.float32
BF16 = jnp.bfloat16
ACT_DTYPE = BF16

D_MODEL = 1024
EPS = 1e-5

SSD_HEADS = 16
SSD_HEAD_DIM = 64
SSD_INNER = SSD_HEADS * SSD_HEAD_DIM
SSD_GROUPS = 4
SSD_STATE = 128
SSD_CONV = 4
SSD_CHUNK = 128
SSD_XBC = SSD_INNER + 2 * SSD_GROUPS * SSD_STATE

CONF_DIM = 1024
CONF_KERNEL = 31

S5_DIM = 512
S5_GROUP = 16
S5_GROUPS = S5_DIM // S5_GROUP
S5_STATE = 64
S5_LANES = S5_GROUPS * S5_STATE
S5_HALF = S5_LANES // 2

RET_HEADS = 4
RET_KEY_DIM = 256
RET_VAL_DIM = 256
RET_CHUNK = 128
RET_QK = RET_HEADS * RET_KEY_DIM
RET_V = RET_HEADS * RET_VAL_DIM
ROPE_BASE = 10000.0

LANES = 128
SUBLANES = 8
VMEM_LIMIT_BYTES = 56 * 1024 * 1024

PROJ_ROWS = 512
PROJ_COLS = 512
FFN_ROWS = 512
FFN_COLS = 256
CONF_ROWS = 256
CONF_HALO = 32
S5_STEPS = 64


def _cparams(*sem):
    return pltpu.CompilerParams(dimension_semantics=sem, vmem_limit_bytes=VMEM_LIMIT_BYTES)


def _dot(a, b):
    return jnp.dot(a, b, preferred_element_type=F32)


def _dot_nt(a, b):
    return lax.dot_general(a, b, (((1,), (1,)), ((), ())), preferred_element_type=F32)


def _split3(x):
    x1 = x.astype(BF16)
    r1 = x - x1.astype(F32)
    x2 = r1.astype(BF16)
    x3 = (r1 - x2.astype(F32)).astype(BF16)
    return x1, x2, x3


def _rms(x, g):
    return x * lax.rsqrt(jnp.mean(x * x, axis=-1, keepdims=True) + EPS) * g


def _norm_proj_kernel(*refs, has_aux):
    if has_aux:
        x_ref, g_ref, w_ref, waux_ref, o_ref, oaux_ref = refs
    else:
        x_ref, g_ref, w_ref, o_ref = refs
    h = _rms(x_ref[...], g_ref[...]).astype(BF16)
    for k in range(w_ref.shape[1] // PROJ_COLS):
        sl = slice(k * PROJ_COLS, (k + 1) * PROJ_COLS)
        o_ref[:, sl] = _dot(h, w_ref[:, sl]).astype(o_ref.dtype)
    if has_aux:
        oaux_ref[...] = _dot(h, waux_ref[...])


def _norm_proj(x, g, w, w_aux, *, name):
    m, d = x.shape
    n = w.shape[1]
    tm = PROJ_ROWS
    has_aux = w_aux is not None
    resident = dict(pipeline_mode=pl.Buffered(1))
    in_specs = [pl.BlockSpec((tm, d), lambda i: (i, 0)),
                pl.BlockSpec((1, d), lambda i: (0, 0)),
                pl.BlockSpec((d, n), lambda i: (0, 0), **resident)]
    out_specs = [pl.BlockSpec((tm, n), lambda i: (i, 0))]
    out_shape = [jax.ShapeDtypeStruct((m, n), ACT_DTYPE)]
    args = [x, g.reshape(1, d), w]
    if has_aux:
        in_specs.append(pl.BlockSpec((d, LANES), lambda i: (0, 0), **resident))
        out_specs.append(pl.BlockSpec((tm, LANES), lambda i: (i, 0)))
        out_shape.append(jax.ShapeDtypeStruct((m, LANES), F32))
        args.append(w_aux)
    return pl.pallas_call(
        functools.partial(_norm_proj_kernel, has_aux=has_aux),
        grid=(m // tm,),
        in_specs=in_specs, out_specs=out_specs, out_shape=out_shape,
        compiler_params=_cparams("parallel"),
        name=name,
    )(*args)


def _pair_lanes(v, j, low):
    return jnp.where(low, v[:, 2 * j:2 * j + 1], v[:, 2 * j + 1:2 * j + 2])


def _ssd_kernel(xbc_ref, z_ref, dt_ref, cw_ref, cb_ref, dtb_ref, alog_ref, dskip_ref, ng_ref,
                o_ref, sh_ref, state_ref):
    L = SSD_CHUNK
    N = SSD_STATE

    @pl.when(pl.program_id(1) == 0)
    def _():
        sh_ref[...] = jnp.zeros(sh_ref.shape, F32)
        state_ref[...] = jnp.zeros(state_ref.shape, F32)

    x_in = xbc_ref[...].astype(F32)
    acc = cb_ref[...] + cw_ref[SSD_CONV - 1:SSD_CONV, :] * x_in
    for d in range(1, SSD_CONV):
        sh_ref[d - 1, SUBLANES:2 * SUBLANES, :] = sh_ref[d - 1, L + SUBLANES:L + 2 * SUBLANES, :]
        sh_ref[d - 1, pl.ds(SUBLANES + d, L), :] = x_in
        acc = acc + cw_ref[SSD_CONV - 1 - d:SSD_CONV - d, :] * sh_ref[d - 1, SUBLANES:SUBLANES + L, :]
    xbc = acc * jax.nn.sigmoid(acc)
    xs = xbc[:, :SSD_INNER]
    bm = xbc[:, SSD_INNER:SSD_INNER + SSD_GROUPS * N]
    cm = xbc[:, SSD_INNER + SSD_GROUPS * N:]

    lane = lax.broadcasted_iota(jnp.int32, (L, LANES), 1)
    dt = jax.nn.softplus(dt_ref[...] + dtb_ref[...])
    a = jnp.where(lane < SSD_HEADS, dt * (-jnp.exp(alog_ref[...])), 0.0)
    row = lax.broadcasted_iota(jnp.int32, (L, L), 0)
    col = lax.broadcasted_iota(jnp.int32, (L, L), 1)
    tril = row >= col
    tri = tril.astype(BF16)
    a1, a2, a3 = _split3(a)
    a_cs = _dot(tri, a1) + _dot(tri, a2) + _dot(tri, a3)
    a_cs_t = a_cs.T
    ea = jnp.exp(a_cs)
    a_last = a_cs[L - 1:L, :]
    ds = jnp.exp(a_last - a_cs)
    ea_last = jnp.exp(a_last)

    low = lane < SSD_HEAD_DIM
    y_tiles = []
    for g in range(SSD_GROUPS):
        bg = bm[:, g * N:(g + 1) * N]
        cg = cm[:, g * N:(g + 1) * N].astype(BF16)
        bg_t = bg.T.astype(BF16)
        cb = _dot(cg, bg_t)
        prev = state_ref[g]
        y_off = _dot(cg, prev.astype(BF16))
        xd_tiles = []
        ea_last_tiles = []
        for jj in range(2):
            j = 2 * g + jj
            dt_p = _pair_lanes(dt, j, low)
            ea_p = _pair_lanes(ea, j, low)
            ds_p = _pair_lanes(ds, j, low)
            xs_p = xs[:, j * LANES:(j + 1) * LANES]
            xdt_p = xs_p * dt_p
            scores = []
            for h in (2 * j, 2 * j + 1):
                diff = a_cs[:, h:h + 1] - a_cs_t[h:h + 1, :]
                lmat = jnp.exp(jnp.where(tril, diff, -jnp.inf))
                scores.append((cb * lmat).astype(BF16))
            lhs = jnp.concatenate(scores, axis=1)
            rhs = jnp.concatenate([jnp.where(low, xdt_p, 0.0), jnp.where(low, 0.0, xdt_p)],
                                  axis=0).astype(BF16)
            y_diag = _dot(lhs, rhs)
            y_p = (y_diag + y_off[:, jj * LANES:(jj + 1) * LANES] * ea_p
                   + dskip_ref[:, j * LANES:(j + 1) * LANES] * xs_p)
            y_tiles.append(y_p)
            xd_tiles.append((xdt_p * ds_p).astype(BF16))
            ea_last_tiles.append(jnp.where(low[0:1, :], ea_last[:, 2 * j:2 * j + 1], ea_last[:, 2 * j + 1:2 * j + 2]))
        xd = jnp.concatenate(xd_tiles, axis=1)
        decay = jnp.concatenate(ea_last_tiles, axis=1)
        state_ref[g] = prev * decay + _dot(bg_t, xd)

    y = jnp.concatenate(y_tiles, axis=1)
    z = z_ref[...].astype(F32)
    y = y * (z * jax.nn.sigmoid(z))
    gw = SSD_INNER // SSD_GROUPS
    outs = []
    for g in range(SSD_GROUPS):
        yg = y[:, g * gw:(g + 1) * gw]
        outs.append(yg * lax.rsqrt(jnp.mean(yg * yg, axis=-1, keepdims=True) + EPS))
    o_ref[...] = (jnp.concatenate(outs, axis=1) * ng_ref[...]).astype(o_ref.dtype)


def _ssd_mixer(u_wide, dt_raw, conv_w, conv_b, dt_bias, a_log, d_skip, norm_g, batch, seq):
    L = SSD_CHUNK
    nc = seq // L
    m = batch * seq

    def pad_heads(v):
        return jnp.pad(v.astype(F32), (0, LANES - SSD_HEADS)).reshape(1, LANES)

    dskip = jnp.repeat(d_skip.astype(F32), SSD_HEAD_DIM).reshape(1, SSD_INNER)
    return pl.pallas_call(
        _ssd_kernel,
        grid=(batch, nc),
        in_specs=[
            pl.BlockSpec((L, SSD_XBC), lambda b, c: (b * nc + c, 0)),
            pl.BlockSpec((L, SSD_INNER), lambda b, c: (b * nc + c, 2 * SSD_XBC // SSD_INNER)),
            pl.BlockSpec((L, LANES), lambda b, c: (b * nc + c, 0)),
            pl.BlockSpec((SSD_CONV, SSD_XBC), lambda b, c: (0, 0)),
            pl.BlockSpec((1, SSD_XBC), lambda b, c: (0, 0)),
            pl.BlockSpec((1, LANES), lambda b, c: (0, 0)),
            pl.BlockSpec((1, LANES), lambda b, c: (0, 0)),
            pl.BlockSpec((1, SSD_INNER), lambda b, c: (0, 0)),
            pl.BlockSpec((1, SSD_INNER), lambda b, c: (0, 0)),
        ],
        out_specs=pl.BlockSpec((L, SSD_INNER), lambda b, c: (b * nc + c, 0)),
        out_shape=jax.ShapeDtypeStruct((m, SSD_INNER), ACT_DTYPE),
        scratch_shapes=[pltpu.VMEM((SSD_CONV - 1, L + 2 * SUBLANES, SSD_XBC), F32),
                        pltpu.VMEM((SSD_GROUPS, SSD_STATE, SSD_INNER // SSD_GROUPS), F32)],
        compiler_params=_cparams("parallel", "arbitrary"),
        name="ssd_mixer",
    )(u_wide, u_wide, dt_raw, conv_w.astype(F32), conv_b.astype(F32).reshape(1, SSD_XBC),
      pad_heads(dt_bias), pad_heads(a_log), dskip, norm_g.astype(F32).reshape(1, SSD_INNER))


def _conf_kernel(ag_ref, w_ref, b_ref, g_ref, beta_ref, o_ref, sh_ref):
    T = CONF_ROWS
    H = CONF_HALO
    strips = CONF_DIM // LANES

    @pl.when(pl.program_id(1) == 0)
    def _():
        sh_ref[...] = jnp.zeros(sh_ref.shape, F32)

    a = ag_ref[:, :CONF_DIM].astype(F32)
    gate = ag_ref[:, CONF_DIM:].astype(F32)
    h = a * jax.nn.sigmoid(gate)
    for s in range(SUBLANES):
        for c in range(strips):
            sh_ref[s, c, 0:H, :] = sh_ref[s, c, T:T + H, :]
            sh_ref[s, c, pl.ds(H - s, T), :] = h[:, c * LANES:(c + 1) * LANES]

    rows = 32
    base = H - (CONF_KERNEL - 1)

    def row_block(r, carry):
        r0 = pl.multiple_of(r * rows, rows)
        accs = []
        for c in range(strips):
            cs = slice(c * LANES, (c + 1) * LANES)
            nt = rows // SUBLANES
            tiles = [jnp.broadcast_to(b_ref[:, cs], (SUBLANES, LANES))] * nt
            for s in range(SUBLANES):
                taps = [(j, (base + j) // SUBLANES) for j in range(CONF_KERNEL) if (base + j) % SUBLANES == s]
                qs = [q for _, q in taps]
                win = {k: sh_ref[s, c, pl.ds(r0 + k * SUBLANES, SUBLANES), :]
                       for k in range(min(qs), max(qs) + nt)}
                for j, q in taps:
                    wj = jnp.broadcast_to(w_ref[j:j + 1, cs], (SUBLANES, LANES))
                    tiles = [t + wj * win[q + i] for i, t in enumerate(tiles)]
            accs.append(jnp.concatenate(tiles, axis=0))
        mu = jnp.sum(sum(accs), axis=-1, keepdims=True) * (1.0 / CONF_DIM)
        cen = [v - mu for v in accs]
        var = jnp.sum(sum(v * v for v in cen), axis=-1, keepdims=True) * (1.0 / CONF_DIM)
        inv = lax.rsqrt(var + EPS)
        for c in range(strips):
            cs = slice(c * LANES, (c + 1) * LANES)
            hn = cen[c] * inv * g_ref[:, cs] + beta_ref[:, cs]
            o_ref[pl.ds(r0, rows), cs] = (hn * jax.nn.sigmoid(hn)).astype(o_ref.dtype)
        return carry

    lax.fori_loop(0, T // rows, row_block, 0, unroll=2)


def _conf_mixer(u_wide, conv_w, conv_b, ln_g, ln_b, batch, seq):
    T = CONF_ROWS
    nt = seq // T
    m = batch * seq
    vec = lambda v: v.astype(F32).reshape(1, CONF_DIM)
    return pl.pallas_call(
        _conf_kernel,
        grid=(batch, nt),
        in_specs=[
            pl.BlockSpec((T, 2 * CONF_DIM), lambda b, c: (b * nt + c, 1)),
            pl.BlockSpec((CONF_KERNEL, CONF_DIM), lambda b, c: (0, 0)),
            pl.BlockSpec((1, CONF_DIM), lambda b, c: (0, 0)),
            pl.BlockSpec((1, CONF_DIM), lambda b, c: (0, 0)),
            pl.BlockSpec((1, CONF_DIM), lambda b, c: (0, 0)),
        ],
        out_specs=pl.BlockSpec((T, CONF_DIM), lambda b, c: (b * nt + c, 0)),
        out_shape=jax.ShapeDtypeStruct((m, CONF_DIM), ACT_DTYPE),
        scratch_shapes=[pltpu.VMEM((SUBLANES, CONF_DIM // LANES, T + CONF_HALO, LANES), F32)],
        compiler_params=_cparams("parallel", "arbitrary"),
        name="conformer_mixer",
    )(u_wide, conv_w.astype(F32), vec(conv_b), vec(ln_g), vec(ln_b))


def _out_proj_kernel(x_ref, a_ref, b_ref, wa_ref, wb_ref, o_ref):
    o_ref[...] = x_ref[...] + _dot(a_ref[...], wa_ref[...]) + _dot(b_ref[...], wb_ref[...])


def _out_proj(x, a, b, w, name):
    m, d = x.shape
    ka, kb = a.shape[1], b.shape[1]
    tm = PROJ_ROWS
    wa = w[:ka].astype(BF16)
    wb = w[ka:].astype(BF16)
    return pl.pallas_call(
        _out_proj_kernel,
        grid=(m // tm,),
        in_specs=[pl.BlockSpec((tm, d), lambda i: (i, 0)),
                  pl.BlockSpec((tm, ka), lambda i: (i, 0)),
                  pl.BlockSpec((tm, kb), lambda i: (i, 0)),
                  pl.BlockSpec((ka, d), lambda i: (0, 0)),
                  pl.BlockSpec((kb, d), lambda i: (0, 0))],
        out_specs=pl.BlockSpec((tm, d), lambda i: (i, 0)),
        out_shape=jax.ShapeDtypeStruct((m, d), F32),
        compiler_params=_cparams("parallel"),
        name=name,
    )(x, a, b, wa, wb)


def _ffn_kernel(x_ref, g_ref, wg_ref, wu_ref, wd_ref, fg_ref, o_ref, act_ref, *, final_norm):
    x = x_ref[...]
    h = _rms(x, g_ref[...]).astype(BF16)
    for k in range(wg_ref.shape[1] // FFN_COLS):
        sl = slice(k * FFN_COLS, (k + 1) * FFN_COLS)
        gate = _dot(h, wg_ref[:, sl])
        up = _dot(h, wu_ref[:, sl])
        act_ref[:, sl] = (gate * jax.nn.sigmoid(gate) * up).astype(BF16)
    y = x + _dot(act_ref[...], wd_ref[...])
    if final_norm:
        y = _rms(y, fg_ref[...])
    o_ref[...] = y


def _ffn(x, g, w_gate, w_up, w_down, final_g, name):
    m, d = x.shape
    f = w_gate.shape[1]
    tm = FFN_ROWS
    final_norm = final_g is not None
    fg = (final_g if final_norm else jnp.ones((d,), F32)).astype(F32).reshape(1, d)
    resident = dict(pipeline_mode=pl.Buffered(1))
    return pl.pallas_call(
        functools.partial(_ffn_kernel, final_norm=final_norm),
        grid=(m // tm,),
        in_specs=[pl.BlockSpec((tm, d), lambda i: (i, 0)),
                  pl.BlockSpec((1, d), lambda i: (0, 0)),
                  pl.BlockSpec((d, f), lambda i: (0, 0), **resident),
                  pl.BlockSpec((d, f), lambda i: (0, 0), **resident),
                  pl.BlockSpec((f, d), lambda i: (0, 0), **resident),
                  pl.BlockSpec((1, d), lambda i: (0, 0))],
        out_specs=pl.BlockSpec((tm, d), lambda i: (i, 0)),
        out_shape=jax.ShapeDtypeStruct((m, d), F32),
        scratch_shapes=[pltpu.VMEM((tm, f), BF16)],
        compiler_params=_cparams("parallel"),
        name=name,
    )(x, g.astype(F32).reshape(1, d), w_gate.astype(BF16), w_up.astype(BF16), w_down.astype(BF16), fg)


def _s5_kernel(u_ref, perm_ref, permt_ref, bre_ref, bim_ref, cre_ref, cim_ref, lre_ref, lim_ref, d_ref,
               gw_ref, gb_ref, o_ref, sre_ref, sim_ref, st_ref):
    nb, ts, dim = u_ref.shape
    rows = nb * ts
    half_in = dim // 2

    @pl.when(pl.program_id(0) == 0)
    def _():
        st_ref[...] = jnp.zeros(st_ref.shape, F32)

    perm = perm_ref[...]
    u_bm = u_ref[...].reshape(rows, dim)
    u_tm = _dot(perm, u_bm)
    u16 = u_tm.astype(BF16)

    for hf in range(2):
        uh = u16[:, hf * half_in:(hf + 1) * half_in]
        sre_ref[:, hf * S5_HALF:(hf + 1) * S5_HALF] = _dot(uh, bre_ref[hf])
        sim_ref[:, hf * S5_HALF:(hf + 1) * S5_HALF] = _dot(uh, bim_ref[hf])

    for hf in range(2):
        sl = slice(hf * S5_HALF, (hf + 1) * S5_HALF)
        lr = lre_ref[:, sl]
        li = lim_ref[:, sl]

        def step(t, carry, sl=sl, lr=lr, li=li):
            sr, si = carry
            r0 = pl.multiple_of(t * nb, nb)
            nr = lr * sr - li * si + sre_ref[pl.ds(r0, nb), sl]
            ni = lr * si + li * sr + sim_ref[pl.ds(r0, nb), sl]
            sre_ref[pl.ds(r0, nb), sl] = nr
            sim_ref[pl.ds(r0, nb), sl] = ni
            return nr, ni

        sr, si = lax.fori_loop(0, ts, step, (st_ref[0, :, sl], st_ref[1, :, sl]), unroll=2)
        st_ref[0, :, sl] = sr
        st_ref[1, :, sl] = si

    ys = []
    for hf in range(2):
        sl = slice(hf * S5_HALF, (hf + 1) * S5_HALF)
        ys.append(_dot(sre_ref[:, sl].astype(BF16), cre_ref[hf]) + _dot(sim_ref[:, sl].astype(BF16), cim_ref[hf]))
    y = jnp.concatenate(ys, axis=1) + d_ref[...] * u_tm
    y = jax.nn.gelu(y)
    y = y * jax.nn.sigmoid(_dot(y.astype(BF16), gw_ref[...]) + gb_ref[...])
    y_bm = _dot(permt_ref[...], y.astype(BF16))
    o_ref[...] = y_bm.reshape(nb, ts, dim).astype(o_ref.dtype)


def _s5_mixer(u_wide3, lam_re, lam_im, b_re, b_im, c_re, c_im, log_dt, d_skip, glu_w, glu_b, batch, seq):
    ts = S5_STEPS
    rows = batch * ts
    G, P, C = S5_GROUPS, S5_STATE, S5_GROUP
    hg = G // 2
    lam = lax.complex(lam_re.astype(F32), lam_im.astype(F32))
    dt = jnp.exp(log_dt.astype(F32))[:, None]
    lam_bar = jnp.exp(lam * dt)
    b_bar = ((lam_bar - 1.0) / lam)[..., None] * lax.complex(b_re.astype(F32), b_im.astype(F32))
    eye = jnp.eye(hg, dtype=F32)

    def blockdiag_in(v):
        blk = v.transpose(0, 2, 1).reshape(2, hg, C, P)
        return jnp.einsum('hgcp,gk->hgckp', blk, eye).reshape(2, hg * C, hg * P).astype(BF16)

    def blockdiag_out(v):
        blk = v.transpose(0, 2, 1).reshape(2, hg, P, C)
        return jnp.einsum('hgpc,gk->hgpkc', blk, eye).reshape(2, hg * P, hg * C).astype(BF16)

    bre = blockdiag_in(jnp.real(b_bar))
    bim = blockdiag_in(jnp.imag(b_bar))
    cre = blockdiag_out(c_re.astype(F32))
    cim = blockdiag_out(-c_im.astype(F32))
    lre = jnp.broadcast_to(jnp.real(lam_bar).reshape(1, G * P), (batch, G * P))
    lim = jnp.broadcast_to(jnp.imag(lam_bar).reshape(1, G * P), (batch, G * P))
    r = np.arange(rows)
    perm_np = np.zeros((rows, rows), np.float32)
    perm_np[r, (r % batch) * ts + r // batch] = 1.0
    perm = jnp.asarray(perm_np, dtype=BF16)
    perm_t = jnp.asarray(perm_np.T, dtype=BF16)
    ublk = u_wide3.shape[2] // S5_DIM - 1
    const2 = lambda i: (0, 0)
    const3 = lambda i: (0, 0, 0)
    return pl.pallas_call(
        _s5_kernel,
        grid=(seq // ts,),
        in_specs=[
            pl.BlockSpec((batch, ts, S5_DIM), lambda i: (0, i, ublk)),
            pl.BlockSpec((rows, rows), const2),
            pl.BlockSpec((rows, rows), const2),
            pl.BlockSpec((2, hg * C, hg * P), const3),
            pl.BlockSpec((2, hg * C, hg * P), const3),
            pl.BlockSpec((2, hg * P, hg * C), const3),
            pl.BlockSpec((2, hg * P, hg * C), const3),
            pl.BlockSpec((batch, G * P), const2),
            pl.BlockSpec((batch, G * P), const2),
            pl.BlockSpec((1, S5_DIM), const2),
            pl.BlockSpec((S5_DIM, S5_DIM), const2),
            pl.BlockSpec((1, S5_DIM), const2),
        ],
        out_specs=pl.BlockSpec((batch, ts, S5_DIM), lambda i: (0, i, 0)),
        out_shape=jax.ShapeDtypeStruct((batch, seq, S5_DIM), ACT_DTYPE),
        scratch_shapes=[pltpu.VMEM((rows, S5_LANES), F32), pltpu.VMEM((rows, S5_LANES), F32),
                        pltpu.VMEM((2, batch, S5_LANES), F32)],
        compiler_params=_cparams("arbitrary"),
        name="s5_mixer",
    )(u_wide3, perm, perm_t, bre, bim, cre, cim, lre, lim, d_skip.astype(F32).reshape(1, S5_DIM),
      glu_w.astype(BF16), glu_b.astype(F32).reshape(1, S5_DIM))


def _ret_kernel(q_ref, k_ref, v_ref, gate_ref, cos_ref, sin_ref, dmat_ref, qdec_ref, kdec_ref, cg_ref,
                gng_ref, gnb_ref, o_ref, state_ref):
    L = RET_CHUNK
    dk, dv = RET_KEY_DIM, RET_VAL_DIM
    half = dk // 2

    @pl.when(pl.program_id(1) == 0)
    def _():
        state_ref[...] = jnp.zeros(state_ref.shape, F32)

    cos = cos_ref[...]
    sin = sin_ref[...]

    def rot(t):
        t1, t2 = t[:, :half], t[:, half:]
        return jnp.concatenate([t1 * cos - t2 * sin, t1 * sin + t2 * cos], axis=1)

    for h in range(RET_HEADS):
        q = rot(q_ref[:, h * dk:(h + 1) * dk].astype(F32))
        k = rot(k_ref[:, h * dk:(h + 1) * dk].astype(F32)) * (dk ** -0.5)
        v = v_ref[:, h * dv:(h + 1) * dv]
        q16 = q.astype(BF16)
        k_t = k.T
        scores = _dot(q16, k_t.astype(BF16)) * dmat_ref[h]
        prev = state_ref[h]
        o = _dot(scores.astype(BF16), v) + _dot(q16, prev.astype(BF16)) * qdec_ref[:, h:h + 1]
        kv = _dot((k_t * kdec_ref[h:h + 1, :]).astype(BF16), v)
        state_ref[h] = prev * cg_ref[h:h + 1, :] + kv
        mu = jnp.mean(o, axis=-1, keepdims=True)
        var = jnp.mean(jnp.square(o - mu), axis=-1, keepdims=True)
        on = (o - mu) * lax.rsqrt(var + EPS) * gng_ref[:, h * dv:(h + 1) * dv] + gnb_ref[:, h * dv:(h + 1) * dv]
        gt = gate_ref[:, h * dv:(h + 1) * dv].astype(F32)
        o_ref[:, h * dv:(h + 1) * dv] = (gt * jax.nn.sigmoid(gt) * on).astype(o_ref.dtype)


def _ret_mixer(u_wide, gn_g, gn_b, batch, seq):
    L = RET_CHUNK
    nc = seq // L
    m = batch * seq
    inv_freq = ROPE_BASE ** (-jnp.arange(0, RET_KEY_DIM, 2, dtype=F32) / RET_KEY_DIM)
    ang = jnp.arange(seq, dtype=F32)[:, None] * inv_freq[None, :]
    cos, sin = jnp.cos(ang), jnp.sin(ang)
    gamma = 1.0 - jnp.exp(jnp.linspace(math.log(1.0 / 32), math.log(1.0 / 512), RET_HEADS, dtype=F32))
    log_g = jnp.log(gamma)
    idx = jnp.arange(L, dtype=F32)
    diff = idx[:, None] - idx[None, :]
    dmat = jnp.where(diff >= 0, jnp.exp(log_g[:, None, None] * jnp.maximum(diff, 0.0)), 0.0)
    kdec = jnp.exp(log_g[:, None] * (L - 1.0 - idx)[None, :])
    kdec = jnp.pad(kdec, ((0, SUBLANES - RET_HEADS), (0, 0)))
    qdec = jnp.exp(log_g[None, :] * (idx + 1.0)[:, None])
    qdec = jnp.pad(qdec, ((0, 0), (0, LANES - RET_HEADS)))
    cg = jnp.broadcast_to(jnp.exp(log_g * L)[:, None], (RET_HEADS, RET_VAL_DIM))
    cg = jnp.pad(cg, ((0, SUBLANES - RET_HEADS), (0, 0)))
    row = lambda b, c: b * nc + c
    return pl.pallas_call(
        _ret_kernel,
        grid=(batch, nc),
        in_specs=[
            pl.BlockSpec((L, RET_QK), lambda b, c: (row(b, c), 0)),
            pl.BlockSpec((L, RET_QK), lambda b, c: (row(b, c), 1)),
            pl.BlockSpec((L, RET_V), lambda b, c: (row(b, c), 2)),
            pl.BlockSpec((L, RET_V), lambda b, c: (row(b, c), 3)),
            pl.BlockSpec((L, RET_KEY_DIM // 2), lambda b, c: (c, 0)),
            pl.BlockSpec((L, RET_KEY_DIM // 2), lambda b, c: (c, 0)),
            pl.BlockSpec((RET_HEADS, L, L), lambda b, c: (0, 0, 0)),
            pl.BlockSpec((L, LANES), lambda b, c: (0, 0)),
            pl.BlockSpec((SUBLANES, L), lambda b, c: (0, 0)),
            pl.BlockSpec((SUBLANES, RET_VAL_DIM), lambda b, c: (0, 0)),
            pl.BlockSpec((1, RET_V), lambda b, c: (0, 0)),
            pl.BlockSpec((1, RET_V), lambda b, c: (0, 0)),
        ],
        out_specs=pl.BlockSpec((L, RET_V), lambda b, c: (row(b, c), 0)),
        out_shape=jax.ShapeDtypeStruct((m, RET_V), ACT_DTYPE),
        scratch_shapes=[pltpu.VMEM((RET_HEADS, RET_KEY_DIM, RET_VAL_DIM), F32)],
        compiler_params=_cparams("parallel", "arbitrary"),
        name="retention_mixer",
    )(u_wide, u_wide, u_wide, u_wide, cos, sin, dmat, qdec, kdec, cg,
      gn_g.astype(F32).reshape(1, RET_V), gn_b.astype(F32).reshape(1, RET_V))


def kernel(x, ev_norm, ev_w_in, ev_conv_w, ev_conv_b, ev_dt_bias, ev_a_log, ev_d, ev_ssd_norm, ev_cf_conv_w, ev_cf_conv_b, ev_cf_ln_g, ev_cf_ln_b, ev_w_out, od_norm, od_w_in, od_lam_re, od_lam_im, od_b_re, od_b_im, od_c_re, od_c_im, od_log_dt, od_s5_d, od_glu_w, od_glu_b, od_gn_g, od_gn_b, od_w_out, ffn_norm, ffn_w_gate, ffn_w_up, ffn_w_down, final_norm):
    batch, seq, d = x.shape
    m = batch * seq
    xf = x.reshape(m, d)

    w = ev_w_in[0]
    o1 = SSD_INNER
    o2 = o1 + SSD_XBC
    o3 = o2 + SSD_HEADS
    w_main = jnp.concatenate([w[:, o1:o2], w[:, o3:], w[:, :o1]], axis=1).astype(BF16)
    w_dt = jnp.pad(w[:, o2:o3], ((0, 0), (0, LANES - SSD_HEADS))).astype(BF16)
    u_wide, dt_raw = _norm_proj(xf, ev_norm[0], w_main, w_dt, name="even_in_proj")
    ya = _ssd_mixer(u_wide, dt_raw, ev_conv_w[0], ev_conv_b[0], ev_dt_bias[0], ev_a_log[0], ev_d[0],
                    ev_ssd_norm[0], batch, seq)
    yb = _conf_mixer(u_wide, ev_cf_conv_w[0], ev_cf_conv_b[0], ev_cf_ln_g[0], ev_cf_ln_b[0], batch, seq)
    xf = _out_proj(xf, ya, yb, ev_w_out[0], "even_out_proj")
    xf = _ffn(xf, ffn_norm[0], ffn_w_gate[0], ffn_w_up[0], ffn_w_down[0], None, "ffn0")

    w = od_w_in[0]
    w_main = jnp.concatenate([w[:, S5_DIM:], w[:, :S5_DIM]], axis=1).astype(BF16)
    (u_wide,) = _norm_proj(xf, od_norm[0], w_main, None, name="odd_in_proj")
    yc = _s5_mixer(u_wide.reshape(batch, seq, -1), od_lam_re[0], od_lam_im[0], od_b_re[0], od_b_im[0],
                   od_c_re[0], od_c_im[0], od_log_dt[0], od_s5_d[0], od_glu_w[0], od_glu_b[0], batch, seq)
    yd = _ret_mixer(u_wide, od_gn_g[0], od_gn_b[0], batch, seq)
    xf = _out_proj(xf, yc.reshape(m, S5_DIM), yd, od_w_out[0], "odd_out_proj")
    xf = _ffn(xf, ffn_norm[1], ffn_w_gate[1], ffn_w_up[1], ffn_w_down[1], final_norm, "ffn1")
    return xf.reshape(batch, seq, d)
```

```python
import functools
import math

import numpy as np
import jax
import jax.numpy as jnp
from jax import lax
from jax.experimental import pallas as pl
from jax.experimental.pallas import tpu as pltpu

F32 = jnp.float32
BF16 = jnp.bfloat16
ACT_DTYPE = BF16

D_MODEL = 1024
EPS = 1e-5

SSD_HEADS = 16
SSD_HEAD_DIM = 64
SSD_INNER = SSD_HEADS * SSD_HEAD_DIM
SSD_GROUPS = 4
SSD_STATE = 128
SSD_CONV = 4
SSD_CHUNK = 128
SSD_XBC = SSD_INNER + 2 * SSD_GROUPS * SSD_STATE

CONF_DIM = 1024
CONF_KERNEL = 31

S5_DIM = 512
S5_GROUP = 16
S5_GROUPS = S5_DIM // S5_GROUP
S5_STATE = 64
S5_LANES = S5_GROUPS * S5_STATE
S5_HALF = S5_LANES // 2

RET_HEADS = 4
RET_KEY_DIM = 256
RET_VAL_DIM = 256
RET_CHUNK = 128
RET_QK = RET_HEADS * RET_KEY_DIM
RET_V = RET_HEADS * RET_VAL_DIM
ROPE_BASE = 10000.0

LANES = 128
SUBLANES = 8
VMEM_LIMIT_BYTES = 56 * 1024 * 1024

PROJ_ROWS = 512
PROJ_COLS = 512
FFN_ROWS = 512
FFN_COLS = 256
CONF_ROWS = 256
CONF_HALO = 32
S5_STEPS = 64
RET_STEP_CHUNKS = 2


def _cparams(*sem):
    return pltpu.CompilerParams(dimension_semantics=sem, vmem_limit_bytes=VMEM_LIMIT_BYTES)


def _dot(a, b):
    return jnp.dot(a, b, preferred_element_type=F32)


def _dot_nt(a, b):
    return lax.dot_general(a, b, (((1,), (1,)), ((), ())), preferred_element_type=F32)


def _split3(x):
    x1 = x.astype(BF16)
    r1 = x - x1.astype(F32)
    x2 = r1.astype(BF16)
    x3 = (r1 - x2.astype(F32)).astype(BF16)
    return x1, x2, x3


def _rms(x, g):
    return x * lax.rsqrt(jnp.mean(x * x, axis=-1, keepdims=True) + EPS) * g


def _norm_proj_kernel(*refs, has_aux):
    if has_aux:
        x_ref, g_ref, w_ref, waux_ref, o_ref, oaux_ref = refs
    else:
        x_ref, g_ref, w_ref, o_ref = refs
    h = _rms(x_ref[...], g_ref[...]).astype(BF16)
    for k in range(w_ref.shape[1] // PROJ_COLS):
        sl = slice(k * PROJ_COLS, (k + 1) * PROJ_COLS)
        o_ref[:, sl] = _dot(h, w_ref[:, sl]).astype(o_ref.dtype)
    if has_aux:
        oaux_ref[...] = _dot(h, waux_ref[...])


def _norm_proj(x, g, w, w_aux, *, name):
    m, d = x.shape
    n = w.shape[1]
    tm = PROJ_ROWS
    has_aux = w_aux is not None
    resident = dict(pipeline_mode=pl.Buffered(1))
    in_specs = [pl.BlockSpec((tm, d), lambda i: (i, 0)),
                pl.BlockSpec((1, d), lambda i: (0, 0)),
                pl.BlockSpec((d, n), lambda i: (0, 0), **resident)]
    out_specs = [pl.BlockSpec((tm, n), lambda i: (i, 0))]
    out_shape = [jax.ShapeDtypeStruct((m, n), ACT_DTYPE)]
    args = [x, g.reshape(1, d), w]
    if has_aux:
        in_specs.append(pl.BlockSpec((d, LANES), lambda i: (0, 0), **resident))
        out_specs.append(pl.BlockSpec((tm, LANES), lambda i: (i, 0)))
        out_shape.append(jax.ShapeDtypeStruct((m, LANES), F32))
        args.append(w_aux)
    return pl.pallas_call(
        functools.partial(_norm_proj_kernel, has_aux=has_aux),
        grid=(m // tm,),
        in_specs=in_specs, out_specs=out_specs, out_shape=out_shape,
        compiler_params=_cparams("parallel"),
        name=name,
    )(*args)


def _pair_lanes(v, j, low):
    return jnp.where(low, v[:, 2 * j:2 * j + 1], v[:, 2 * j + 1:2 * j + 2])


def _ssd_kernel(xbc_ref, z_ref, dt_ref, cw_ref, cb_ref, dtb_ref, alog_ref, dskip_ref, ng_ref,
                o_ref, sh_ref, state_ref):
    L = SSD_CHUNK
    N = SSD_STATE

    @pl.when(pl.program_id(1) == 0)
    def _():
        sh_ref[...] = jnp.zeros(sh_ref.shape, F32)
        state_ref[...] = jnp.zeros(state_ref.shape, F32)

    x_in = xbc_ref[...].astype(F32)
    acc = cb_ref[...] + cw_ref[SSD_CONV - 1:SSD_CONV, :] * x_in
    for d in range(1, SSD_CONV):
        sh_ref[d - 1, SUBLANES:2 * SUBLANES, :] = sh_ref[d - 1, L + SUBLANES:L + 2 * SUBLANES, :]
        sh_ref[d - 1, pl.ds(SUBLANES + d, L), :] = x_in
        acc = acc + cw_ref[SSD_CONV - 1 - d:SSD_CONV - d, :] * sh_ref[d - 1, SUBLANES:SUBLANES + L, :]
    xbc = acc * jax.nn.sigmoid(acc)
    xs = xbc[:, :SSD_INNER]
    bm = xbc[:, SSD_INNER:SSD_INNER + SSD_GROUPS * N]
    cm = xbc[:, SSD_INNER + SSD_GROUPS * N:]

    lane = lax.broadcasted_iota(jnp.int32, (L, LANES), 1)
    dt = jax.nn.softplus(dt_ref[...] + dtb_ref[...])
    a = jnp.where(lane < SSD_HEADS, dt * (-jnp.exp(alog_ref[...])), 0.0)
    row = lax.broadcasted_iota(jnp.int32, (L, L), 0)
    col = lax.broadcasted_iota(jnp.int32, (L, L), 1)
    tril = row >= col
    tri = tril.astype(BF16)
    a1, a2, a3 = _split3(a)
    a_cs = _dot(tri, a1) + _dot(tri, a2) + _dot(tri, a3)
    a_cs_t = a_cs.T
    ea = jnp.exp(a_cs)
    a_last = a_cs[L - 1:L, :]
    ds = jnp.exp(a_last - a_cs)
    ea_last = jnp.exp(a_last)

    low = lane < SSD_HEAD_DIM
    y_tiles = []
    for g in range(SSD_GROUPS):
        bg = bm[:, g * N:(g + 1) * N]
        cg = cm[:, g * N:(g + 1) * N].astype(BF16)
        bg_t = bg.T.astype(BF16)
        cb = _dot(cg, bg_t)
        prev = state_ref[g]
        y_off = _dot(cg, prev.astype(BF16))
        xd_tiles = []
        ea_last_tiles = []
        for jj in range(2):
            j = 2 * g + jj
            dt_p = _pair_lanes(dt, j, low)
            ea_p = _pair_lanes(ea, j, low)
            ds_p = _pair_lanes(ds, j, low)
            xs_p = xs[:, j * LANES:(j + 1) * LANES]
            xdt_p = xs_p * dt_p
            scores = []
            for h in (2 * j, 2 * j + 1):
                diff = a_cs[:, h:h + 1] - a_cs_t[h:h + 1, :]
                lmat = jnp.exp(jnp.where(tril, diff, -jnp.inf))
                scores.append((cb * lmat).astype(BF16))
            lhs = jnp.concatenate(scores, axis=1)
            rhs = jnp.concatenate([jnp.where(low, xdt_p, 0.0), jnp.where(low, 0.0, xdt_p)],
                                  axis=0).astype(BF16)
            y_diag = _dot(lhs, rhs)
            y_p = (y_diag + y_off[:, jj * LANES:(jj + 1) * LANES] * ea_p
                   + dskip_ref[:, j * LANES:(j + 1) * LANES] * xs_p)
            y_tiles.append(y_p)
            xd_tiles.append((xdt_p * ds_p).astype(BF16))
            ea_last_tiles.append(jnp.where(low[0:1, :], ea_last[:, 2 * j:2 * j + 1], ea_last[:, 2 * j + 1:2 * j + 2]))
        xd = jnp.concatenate(xd_tiles, axis=1)
        decay = jnp.concatenate(ea_last_tiles, axis=1)
        state_ref[g] = prev * decay + _dot(bg_t, xd)

    y = jnp.concatenate(y_tiles, axis=1)
    z = z_ref[...].astype(F32)
    y = y * (z * jax.nn.sigmoid(z))
    gw = SSD_INNER // SSD_GROUPS
    outs = []
    for g in range(SSD_GROUPS):
        yg = y[:, g * gw:(g + 1) * gw]
        outs.append(yg * lax.rsqrt(jnp.mean(yg * yg, axis=-1, keepdims=True) + EPS))
    o_ref[...] = (jnp.concatenate(outs, axis=1) * ng_ref[...]).astype(o_ref.dtype)


def _ssd_mixer(u_wide, dt_raw, conv_w, conv_b, dt_bias, a_log, d_skip, norm_g, batch, seq):
    L = SSD_CHUNK
    nc = seq // L
    m = batch * seq

    def pad_heads(v):
        return jnp.pad(v.astype(F32), (0, LANES - SSD_HEADS)).reshape(1, LANES)

    dskip = jnp.repeat(d_skip.astype(F32), SSD_HEAD_DIM).reshape(1, SSD_INNER)
    return pl.pallas_call(
        _ssd_kernel,
        grid=(batch, nc),
        in_specs=[
            pl.BlockSpec((L, SSD_XBC), lambda b, c: (b * nc + c, 0)),
            pl.BlockSpec((L, SSD_INNER), lambda b, c: (b * nc + c, 2 * SSD_XBC // SSD_INNER)),
            pl.BlockSpec((L, LANES), lambda b, c: (b * nc + c, 0)),
            pl.BlockSpec((SSD_CONV, SSD_XBC), lambda b, c: (0, 0)),
            pl.BlockSpec((1, SSD_XBC), lambda b, c: (0, 0)),
            pl.BlockSpec((1, LANES), lambda b, c: (0, 0)),
            pl.BlockSpec((1, LANES), lambda b, c: (0, 0)),
            pl.BlockSpec((1, SSD_INNER), lambda b, c: (0, 0)),
            pl.BlockSpec((1, SSD_INNER), lambda b, c: (0, 0)),
        ],
        out_specs=pl.BlockSpec((L, SSD_INNER), lambda b, c: (b * nc + c, 0)),
        out_shape=jax.ShapeDtypeStruct((m, SSD_INNER), ACT_DTYPE),
        scratch_shapes=[pltpu.VMEM((SSD_CONV - 1, L + 2 * SUBLANES, SSD_XBC), F32),
                        pltpu.VMEM((SSD_GROUPS, SSD_STATE, SSD_INNER // SSD_GROUPS), F32)],
        compiler_params=_cparams("parallel", "arbitrary"),
        name="ssd_mixer",
    )(u_wide, u_wide, dt_raw, conv_w.astype(F32), conv_b.astype(F32).reshape(1, SSD_XBC),
      pad_heads(dt_bias), pad_heads(a_log), dskip, norm_g.astype(F32).reshape(1, SSD_INNER))


def _conf_kernel(ag_ref, w_ref, b_ref, g_ref, beta_ref, o_ref, sh_ref, y_ref):
    T = CONF_ROWS
    H = CONF_HALO
    strips = CONF_DIM // LANES

    @pl.when(pl.program_id(1) == 0)
    def _():
        sh_ref[...] = jnp.zeros(sh_ref.shape, F32)

    a = ag_ref[:, :CONF_DIM].astype(F32)
    gate = ag_ref[:, CONF_DIM:].astype(F32)
    h = a * jax.nn.sigmoid(gate)
    for s in range(SUBLANES):
        for c in range(strips):
            sh_ref[s, c, 0:H, :] = sh_ref[s, c, T:T + H, :]
            sh_ref[s, c, pl.ds(H - s, T), :] = h[:, c * LANES:(c + 1) * LANES]

    base = H - (CONF_KERNEL - 1)
    nt = 16
    groups = T // (nt * SUBLANES)

    def strip_group(gi, carry):
        c = gi // groups
        r0 = pl.multiple_of((gi % groups) * (nt * SUBLANES), nt * SUBLANES)
        tiles = [jnp.broadcast_to(b_ref[c], (SUBLANES, LANES))] * nt
        for j in range(CONF_KERNEL):
            q, s = divmod(base + j, SUBLANES)
            wj = jnp.broadcast_to(w_ref[c, j:j + 1, :], (SUBLANES, LANES))
            tiles = [t + wj * sh_ref[s, c, pl.ds(r0 + (q + i) * SUBLANES, SUBLANES), :]
                     for i, t in enumerate(tiles)]
        y_ref[c, pl.ds(r0, nt * SUBLANES), :] = jnp.concatenate(tiles, axis=0)
        return carry

    lax.fori_loop(0, strips * groups, strip_group, 0)

    ys = [y_ref[c] for c in range(strips)]
    mu = jnp.sum(sum(ys), axis=-1, keepdims=True) * (1.0 / CONF_DIM)
    cen = [v - mu for v in ys]
    var = jnp.sum(sum(v * v for v in cen), axis=-1, keepdims=True) * (1.0 / CONF_DIM)
    inv = lax.rsqrt(var + EPS)
    for c in range(strips):
        cs = slice(c * LANES, (c + 1) * LANES)
        hn = cen[c] * inv * g_ref[:, cs] + beta_ref[:, cs]
        o_ref[:, cs] = (hn * jax.nn.sigmoid(hn)).astype(o_ref.dtype)


def _conf_mixer(u_wide, conv_w, conv_b, ln_g, ln_b, batch, seq):
    T = CONF_ROWS
    nt = seq // T
    m = batch * seq
    vec = lambda v: v.astype(F32).reshape(1, CONF_DIM)
    strips = CONF_DIM // LANES
    taps_pad = CONF_KERNEL + 1
    w_strips = jnp.pad(conv_w.astype(F32), ((0, 1), (0, 0))).reshape(taps_pad, strips, LANES).transpose(1, 0, 2)
    b_strips = conv_b.astype(F32).reshape(strips, 1, LANES)
    return pl.pallas_call(
        _conf_kernel,
        grid=(batch, nt),
        in_specs=[
            pl.BlockSpec((T, 2 * CONF_DIM), lambda b, c: (b * nt + c, 1)),
            pl.BlockSpec((strips, taps_pad, LANES), lambda b, c: (0, 0, 0)),
            pl.BlockSpec((strips, 1, LANES), lambda b, c: (0, 0, 0)),
            pl.BlockSpec((1, CONF_DIM), lambda b, c: (0, 0)),
            pl.BlockSpec((1, CONF_DIM), lambda b, c: (0, 0)),
        ],
        out_specs=pl.BlockSpec((T, CONF_DIM), lambda b, c: (b * nt + c, 0)),
        out_shape=jax.ShapeDtypeStruct((m, CONF_DIM), ACT_DTYPE),
        scratch_shapes=[pltpu.VMEM((SUBLANES, CONF_DIM // LANES, T + CONF_HALO, LANES), F32),
                        pltpu.VMEM((CONF_DIM // LANES, T, LANES), F32)],
        compiler_params=_cparams("parallel", "arbitrary"),
        name="conformer_mixer",
    )(u_wide, w_strips, b_strips, vec(ln_g), vec(ln_b))


def _mix_ffn_kernel(x_ref, a_ref, b_ref, wa_ref, wb_ref, g_ref, wg_ref, wu_ref, wd_ref, fg_ref, o_ref, act_ref,
                    *, final_norm):
    x = x_ref[...] + _dot(a_ref[...], wa_ref[...]) + _dot(b_ref[...], wb_ref[...])
    h = _rms(x, g_ref[...]).astype(BF16)
    for k in range(wg_ref.shape[1] // FFN_COLS):
        sl = slice(k * FFN_COLS, (k + 1) * FFN_COLS)
        gate = _dot(h, wg_ref[:, sl])
        up = _dot(h, wu_ref[:, sl])
        act_ref[:, sl] = (gate * jax.nn.sigmoid(gate) * up).astype(BF16)
    y = x + _dot(act_ref[...], wd_ref[...])
    if final_norm:
        y = _rms(y, fg_ref[...])
    o_ref[...] = y


def _mix_ffn(x, a, b, w_out, g, w_gate, w_up, w_down, final_g, name):
    m, d = x.shape
    ka, kb = a.shape[1], b.shape[1]
    f = w_gate.shape[1]
    tm = FFN_ROWS
    final_norm = final_g is not None
    fg = (final_g if final_norm else jnp.ones((d,), F32)).astype(F32).reshape(1, d)
    resident = dict(pipeline_mode=pl.Buffered(1))
    return pl.pallas_call(
        functools.partial(_mix_ffn_kernel, final_norm=final_norm),
        grid=(m // tm,),
        in_specs=[pl.BlockSpec((tm, d), lambda i: (i, 0)),
                  pl.BlockSpec((tm, ka), lambda i: (i, 0)),
                  pl.BlockSpec((tm, kb), lambda i: (i, 0)),
                  pl.BlockSpec((ka, d), lambda i: (0, 0), **resident),
                  pl.BlockSpec((kb, d), lambda i: (0, 0), **resident),
                  pl.BlockSpec((1, d), lambda i: (0, 0)),
                  pl.BlockSpec((d, f), lambda i: (0, 0), **resident),
                  pl.BlockSpec((d, f), lambda i: (0, 0), **resident),
                  pl.BlockSpec((f, d), lambda i: (0, 0), **resident),
                  pl.BlockSpec((1, d), lambda i: (0, 0))],
        out_specs=pl.BlockSpec((tm, d), lambda i: (i, 0)),
        out_shape=jax.ShapeDtypeStruct((m, d), F32),
        scratch_shapes=[pltpu.VMEM((tm, f), BF16)],
        compiler_params=_cparams("parallel"),
        name=name,
    )(x, a, b, w_out[:ka].astype(BF16), w_out[ka:].astype(BF16), g.astype(F32).reshape(1, d),
      w_gate.astype(BF16), w_up.astype(BF16), w_down.astype(BF16), fg)


def _s5_kernel(u_ref, perm_ref, permt_ref, bre_ref, bim_ref, cre_ref, cim_ref, lre_ref, lim_ref, d_ref,
               gw_ref, gb_ref, o_ref, sre_ref, sim_ref, st_ref):
    nb, ts, dim = u_ref.shape
    rows = nb * ts
    half_in = dim // 2

    @pl.when(pl.program_id(0) == 0)
    def _():
        st_ref[...] = jnp.zeros(st_ref.shape, F32)

    perm = perm_ref[...]
    u_bm = u_ref[...].reshape(rows, dim)
    u_tm = _dot(perm, u_bm)
    u16 = u_tm.astype(BF16)

    for hf in range(2):
        uh = u16[:, hf * half_in:(hf + 1) * half_in]
        sre_ref[:, hf * S5_HALF:(hf + 1) * S5_HALF] = _dot(uh, bre_ref[hf])
        sim_ref[:, hf * S5_HALF:(hf + 1) * S5_HALF] = _dot(uh, bim_ref[hf])

    for hf in range(2):
        sl = slice(hf * S5_HALF, (hf + 1) * S5_HALF)
        lr = lre_ref[:, sl]
        li = lim_ref[:, sl]

        def step(t, carry, sl=sl, lr=lr, li=li):
            sr, si = carry
            r0 = pl.multiple_of(t * nb, nb)
            nr = lr * sr - li * si + sre_ref[pl.ds(r0, nb), sl]
            ni = lr * si + li * sr + sim_ref[pl.ds(r0, nb), sl]
            sre_ref[pl.ds(r0, nb), sl] = nr
            sim_ref[pl.ds(r0, nb), sl] = ni
            return nr, ni

        sr, si = lax.fori_loop(0, ts, step, (st_ref[0, :, sl], st_ref[1, :, sl]), unroll=2)
        st_ref[0, :, sl] = sr
        st_ref[1, :, sl] = si

    ys = []
    for hf in range(2):
        sl = slice(hf * S5_HALF, (hf + 1) * S5_HALF)
        ys.append(_dot(sre_ref[:, sl].astype(BF16), cre_ref[hf]) + _dot(sim_ref[:, sl].astype(BF16), cim_ref[hf]))
    y = jnp.concatenate(ys, axis=1) + d_ref[...] * u_tm
    y = jax.nn.gelu(y)
    y = y * jax.nn.sigmoid(_dot(y.astype(BF16), gw_ref[...]) + gb_ref[...])
    y_bm = _dot(permt_ref[...], y.astype(BF16))
    o_ref[...] = y_bm.reshape(nb, ts, dim).astype(o_ref.dtype)


def _s5_mixer(u_wide3, lam_re, lam_im, b_re, b_im, c_re, c_im, log_dt, d_skip, glu_w, glu_b, batch, seq):
    ts = S5_STEPS
    rows = batch * ts
    G, P, C = S5_GROUPS, S5_STATE, S5_GROUP
    hg = G // 2
    lam = lax.complex(lam_re.astype(F32), lam_im.astype(F32))
    dt = jnp.exp(log_dt.astype(F32))[:, None]
    lam_bar = jnp.exp(lam * dt)
    b_bar = ((lam_bar - 1.0) / lam)[..., None] * lax.complex(b_re.astype(F32), b_im.astype(F32))
    eye = jnp.eye(hg, dtype=F32)

    def blockdiag_in(v):
        blk = v.transpose(0, 2, 1).reshape(2, hg, C, P)
        return jnp.einsum('hgcp,gk->hgckp', blk, eye).reshape(2, hg * C, hg * P).astype(BF16)

    def blockdiag_out(v):
        blk = v.transpose(0, 2, 1).reshape(2, hg, P, C)
        return jnp.einsum('hgpc,gk->hgpkc', blk, eye).reshape(2, hg * P, hg * C).astype(BF16)

    bre = blockdiag_in(jnp.real(b_bar))
    bim = blockdiag_in(jnp.imag(b_bar))
    cre = blockdiag_out(c_re.astype(F32))
    cim = blockdiag_out(-c_im.astype(F32))
    lre = jnp.broadcast_to(jnp.real(lam_bar).reshape(1, G * P), (batch, G * P))
    lim = jnp.broadcast_to(jnp.imag(lam_bar).reshape(1, G * P), (batch, G * P))
    r = np.arange(rows)
    perm_np = np.zeros((rows, rows), np.float32)
    perm_np[r, (r % batch) * ts + r // batch] = 1.0
    perm = jnp.asarray(perm_np, dtype=BF16)
    perm_t = jnp.asarray(perm_np.T, dtype=BF16)
    ublk = u_wide3.shape[2] // S5_DIM - 1
    const2 = lambda i: (0, 0)
    const3 = lambda i: (0, 0, 0)
    return pl.pallas_call(
        _s5_kernel,
        grid=(seq // ts,),
        in_specs=[
            pl.BlockSpec((batch, ts, S5_DIM), lambda i: (0, i, ublk)),
            pl.BlockSpec((rows, rows), const2),
            pl.BlockSpec((rows, rows), const2),
            pl.BlockSpec((2, hg * C, hg * P), const3),
            pl.BlockSpec((2, hg * C, hg * P), const3),
            pl.BlockSpec((2, hg * P, hg * C), const3),
            pl.BlockSpec((2, hg * P, hg * C), const3),
            pl.BlockSpec((batch, G * P), const2),
            pl.BlockSpec((batch, G * P), const2),
            pl.BlockSpec((1, S5_DIM), const2),
            pl.BlockSpec((S5_DIM, S5_DIM), const2),
            pl.BlockSpec((1, S5_DIM), const2),
        ],
        out_specs=pl.BlockSpec((batch, ts, S5_DIM), lambda i: (0, i, 0)),
        out_shape=jax.ShapeDtypeStruct((batch, seq, S5_DIM), ACT_DTYPE),
        scratch_shapes=[pltpu.VMEM((rows, S5_LANES), F32), pltpu.VMEM((rows, S5_LANES), F32),
                        pltpu.VMEM((2, batch, S5_LANES), F32)],
        compiler_params=_cparams("arbitrary"),
        name="s5_mixer",
    )(u_wide3, perm, perm_t, bre, bim, cre, cim, lre, lim, d_skip.astype(F32).reshape(1, S5_DIM),
      glu_w.astype(BF16), glu_b.astype(F32).reshape(1, S5_DIM))


def _ret_kernel(q_ref, k_ref, v_ref, gate_ref, cos_ref, sin_ref, dmat_ref, qdec_ref, kdec_ref, cg_ref,
                gng_ref, gnb_ref, o_ref, state_ref):
    L = RET_CHUNK
    dk, dv = RET_KEY_DIM, RET_VAL_DIM
    half = dk // 2

    @pl.when(pl.program_id(1) == 0)
    def _():
        state_ref[...] = jnp.zeros(state_ref.shape, F32)

    def rot(t, cos, sin):
        t1, t2 = t[:, :half], t[:, half:]
        return jnp.concatenate([t1 * cos - t2 * sin, t1 * sin + t2 * cos], axis=1)

    for ci in range(q_ref.shape[0] // L):
        rs = slice(ci * L, (ci + 1) * L)
        cos = cos_ref[rs, :]
        sin = sin_ref[rs, :]
        for h in range(RET_HEADS):
            q = rot(q_ref[rs, h * dk:(h + 1) * dk].astype(F32), cos, sin)
            k = rot(k_ref[rs, h * dk:(h + 1) * dk].astype(F32), cos, sin) * (dk ** -0.5)
            v = v_ref[rs, h * dv:(h + 1) * dv]
            q16 = q.astype(BF16)
            k_t = k.T
            scores = _dot(q16, k_t.astype(BF16)) * dmat_ref[h]
            prev = state_ref[h]
            o = _dot(scores.astype(BF16), v) + _dot(q16, prev.astype(BF16)) * qdec_ref[:, h:h + 1]
            kv = _dot((k_t * kdec_ref[h:h + 1, :]).astype(BF16), v)
            state_ref[h] = prev * cg_ref[h:h + 1, :] + kv
            mu = jnp.mean(o, axis=-1, keepdims=True)
            var = jnp.mean(jnp.square(o - mu), axis=-1, keepdims=True)
            on = ((o - mu) * lax.rsqrt(var + EPS) * gng_ref[:, h * dv:(h + 1) * dv]
                  + gnb_ref[:, h * dv:(h + 1) * dv])
            gt = gate_ref[rs, h * dv:(h + 1) * dv].astype(F32)
            o_ref[rs, h * dv:(h + 1) * dv] = (gt * jax.nn.sigmoid(gt) * on).astype(o_ref.dtype)


def _ret_mixer(u_wide, gn_g, gn_b, batch, seq):
    L = RET_CHUNK
    nc = seq // L
    m = batch * seq
    inv_freq = ROPE_BASE ** (-jnp.arange(0, RET_KEY_DIM, 2, dtype=F32) / RET_KEY_DIM)
    ang = jnp.arange(seq, dtype=F32)[:, None] * inv_freq[None, :]
    cos, sin = jnp.cos(ang), jnp.sin(ang)
    gamma = 1.0 - jnp.exp(jnp.linspace(math.log(1.0 / 32), math.log(1.0 / 512), RET_HEADS, dtype=F32))
    log_g = jnp.log(gamma)
    idx = jnp.arange(L, dtype=F32)
    diff = idx[:, None] - idx[None, :]
    dmat = jnp.where(diff >= 0, jnp.exp(log_g[:, None, None] * jnp.maximum(diff, 0.0)), 0.0)
    kdec = jnp.exp(log_g[:, None] * (L - 1.0 - idx)[None, :])
    kdec = jnp.pad(kdec, ((0, SUBLANES - RET_HEADS), (0, 0)))
    qdec = jnp.exp(log_g[None, :] * (idx + 1.0)[:, None])
    qdec = jnp.pad(qdec, ((0, 0), (0, LANES - RET_HEADS)))
    cg = jnp.broadcast_to(jnp.exp(log_g * L)[:, None], (RET_HEADS, RET_VAL_DIM))
    cg = jnp.pad(cg, ((0, SUBLANES - RET_HEADS), (0, 0)))
    T = RET_STEP_CHUNKS * L
    ns = seq // T
    row = lambda b, c: b * ns + c
    return pl.pallas_call(
        _ret_kernel,
        grid=(batch, ns),
        in_specs=[
            pl.BlockSpec((T, RET_QK), lambda b, c: (row(b, c), 0)),
            pl.BlockSpec((T, RET_QK), lambda b, c: (row(b, c), 1)),
            pl.BlockSpec((T, RET_V), lambda b, c: (row(b, c), 2)),
            pl.BlockSpec((T, RET_V), lambda b, c: (row(b, c), 3)),
            pl.BlockSpec((T, RET_KEY_DIM // 2), lambda b, c: (c, 0)),
            pl.BlockSpec((T, RET_KEY_DIM // 2), lambda b, c: (c, 0)),
            pl.BlockSpec((RET_HEADS, L, L), lambda b, c: (0, 0, 0)),
            pl.BlockSpec((L, LANES), lambda b, c: (0, 0)),
            pl.BlockSpec((SUBLANES, L), lambda b, c: (0, 0)),
            pl.BlockSpec((SUBLANES, RET_VAL_DIM), lambda b, c: (0, 0)),
            pl.BlockSpec((1, RET_V), lambda b, c: (0, 0)),
            pl.BlockSpec((1, RET_V), lambda b, c: (0, 0)),
        ],
        out_specs=pl.BlockSpec((T, RET_V), lambda b, c: (row(b, c), 0)),
        out_shape=jax.ShapeDtypeStruct((m, RET_V), ACT_DTYPE),
        scratch_shapes=[pltpu.VMEM((RET_HEADS, RET_KEY_DIM, RET_VAL_DIM), F32)],
        compiler_params=_cparams("parallel", "arbitrary"),
        name="retention_mixer",
    )(u_wide, u_wide, u_wide, u_wide, cos, sin, dmat, qdec, kdec, cg,
      gn_g.astype(F32).reshape(1, RET_V), gn_b.astype(F32).reshape(1, RET_V))


def kernel(x, ev_norm, ev_w_in, ev_conv_w, ev_conv_b, ev_dt_bias, ev_a_log, ev_d, ev_ssd_norm, ev_cf_conv_w, ev_cf_conv_b, ev_cf_ln_g, ev_cf_ln_b, ev_w_out, od_norm, od_w_in, od_lam_re, od_lam_im, od_b_re, od_b_im, od_c_re, od_c_im, od_log_dt, od_s5_d, od_glu_w, od_glu_b, od_gn_g, od_gn_b, od_w_out, ffn_norm, ffn_w_gate, ffn_w_up, ffn_w_down, final_norm):
    batch, seq, d = x.shape
    m = batch * seq
    xf = x.reshape(m, d)

    w = ev_w_in[0]
    o1 = SSD_INNER
    o2 = o1 + SSD_XBC
    o3 = o2 + SSD_HEADS
    w_main = jnp.concatenate([w[:, o1:o2], w[:, o3:], w[:, :o1]], axis=1).astype(BF16)
    w_dt = jnp.pad(w[:, o2:o3], ((0, 0), (0, LANES - SSD_HEADS))).astype(BF16)
    u_wide, dt_raw = _norm_proj(xf, ev_norm[0], w_main, w_dt, name="even_in_proj")
    ya = _ssd_mixer(u_wide, dt_raw, ev_conv_w[0], ev_conv_b[0], ev_dt_bias[0], ev_a_log[0], ev_d[0],
                    ev_ssd_norm[0], batch, seq)
    yb = _conf_mixer(u_wide, ev_cf_conv_w[0], ev_cf_conv_b[0], ev_cf_ln_g[0], ev_cf_ln_b[0], batch, seq)
    xf = _mix_ffn(xf, ya, yb, ev_w_out[0], ffn_norm[0], ffn_w_gate[0], ffn_w_up[0], ffn_w_down[0], None,
                  "even_out_ffn")

    w = od_w_in[0]
    w_main = jnp.concatenate([w[:, S5_DIM:], w[:, :S5_DIM]], axis=1).astype(BF16)
    (u_wide,) = _norm_proj(xf, od_norm[0], w_main, None, name="odd_in_proj")
    yc = _s5_mixer(u_wide.reshape(batch, seq, -1), od_lam_re[0], od_lam_im[0], od_b_re[0], od_b_im[0],
                   od_c_re[0], od_c_im[0], od_log_dt[0], od_s5_d[0], od_glu_w[0], od_glu_b[0], batch, seq)
    yd = _ret_mixer(u_wide, od_gn_g[0], od_gn_b[0], batch, seq)
    xf = _mix_ffn(xf, yc.reshape(m, S5_DIM), yd, od_w_out[0], ffn_norm[1], ffn_w_gate[1], ffn_w_up[1],
                  ffn_w_down[1], final_norm, "odd_out_ffn")
    return xf.reshape(batch, seq, d)
```

```python
import functools
import math

import numpy as np
import jax
import jax.numpy as jnp
from jax import lax
from jax.experimental import pallas as pl
from jax.experimental.pallas import tpu as pltpu

F32 = jnp.float32
BF16 = jnp.bfloat16
ACT_DTYPE = BF16

D_MODEL = 1024
EPS = 1e-5

SSD_HEADS = 16
SSD_HEAD_DIM = 64
SSD_INNER = SSD_HEADS * SSD_HEAD_DIM
SSD_GROUPS = 4
SSD_STATE = 128
SSD_CONV = 4
SSD_CHUNK = 128
SSD_XBC = SSD_INNER + 2 * SSD_GROUPS * SSD_STATE

CONF_DIM = 1024
CONF_KERNEL = 31

S5_DIM = 512
S5_GROUP = 16
S5_GROUPS = S5_DIM // S5_GROUP
S5_STATE = 64
S5_LANES = S5_GROUPS * S5_STATE
S5_HALF = S5_LANES // 2

RET_HEADS = 4
RET_KEY_DIM = 256
RET_VAL_DIM = 256
RET_CHUNK = 128
RET_QK = RET_HEADS * RET_KEY_DIM
RET_V = RET_HEADS * RET_VAL_DIM
ROPE_BASE = 10000.0

LANES = 128
SUBLANES = 8
VMEM_LIMIT_BYTES = 56 * 1024 * 1024

PROJ_ROWS = 512
PROJ_COLS = 512
FFN_ROWS = 512
FFN_COLS = 256
CONF_ROWS = 256
CONF_HALO = 32
S5_STEPS = 64
RET_STEP_CHUNKS = 2
SSD_STEP_CHUNKS = 4


def _cparams(*sem):
    return pltpu.CompilerParams(dimension_semantics=sem, vmem_limit_bytes=VMEM_LIMIT_BYTES)


def _dot(a, b):
    return jnp.dot(a, b, preferred_element_type=F32)


def _split3(x):
    x1 = x.astype(BF16)
    r1 = x - x1.astype(F32)
    x2 = r1.astype(BF16)
    x3 = (r1 - x2.astype(F32)).astype(BF16)
    return x1, x2, x3


def _rms(x, g):
    return x * lax.rsqrt(jnp.mean(x * x, axis=-1, keepdims=True) + EPS) * g


def _norm_proj_kernel(*refs, has_aux):
    if has_aux:
        x_ref, g_ref, w_ref, waux_ref, o_ref, oaux_ref = refs
    else:
        x_ref, g_ref, w_ref, o_ref = refs
    h = _rms(x_ref[...], g_ref[...]).astype(BF16)
    for k in range(w_ref.shape[1] // PROJ_COLS):
        sl = slice(k * PROJ_COLS, (k + 1) * PROJ_COLS)
        o_ref[:, sl] = _dot(h, w_ref[:, sl]).astype(o_ref.dtype)
    if has_aux:
        oaux_ref[...] = _dot(h, waux_ref[...])


def _norm_proj(x, g, w, w_aux, *, name):
    m, d = x.shape
    n = w.shape[1]
    tm = PROJ_ROWS
    has_aux = w_aux is not None
    resident = dict(pipeline_mode=pl.Buffered(1))
    in_specs = [pl.BlockSpec((tm, d), lambda i: (i, 0)),
                pl.BlockSpec((1, d), lambda i: (0, 0)),
                pl.BlockSpec((d, n), lambda i: (0, 0), **resident)]
    out_specs = [pl.BlockSpec((tm, n), lambda i: (i, 0))]
    out_shape = [jax.ShapeDtypeStruct((m, n), ACT_DTYPE)]
    args = [x, g.reshape(1, d), w]
    if has_aux:
        in_specs.append(pl.BlockSpec((d, LANES), lambda i: (0, 0), **resident))
        out_specs.append(pl.BlockSpec((tm, LANES), lambda i: (i, 0)))
        out_shape.append(jax.ShapeDtypeStruct((m, LANES), F32))
        args.append(w_aux)
    return pl.pallas_call(
        functools.partial(_norm_proj_kernel, has_aux=has_aux),
        grid=(m // tm,),
        in_specs=in_specs, out_specs=out_specs, out_shape=out_shape,
        compiler_params=_cparams("parallel"),
        name=name,
    )(*args)


def _ssd_kernel(xbc_ref, z_ref, dt_ref, cw_ref, cb_ref, dtb_ref, alog_ref, dskip_ref, ng_ref, e_ref,
                o_ref, tail_ref, state_ref):
    L = SSD_CHUNK
    N = SSD_STATE
    gw = SSD_INNER // SSD_GROUPS

    @pl.when(pl.program_id(1) == 0)
    def _():
        tail_ref[...] = jnp.zeros(tail_ref.shape, tail_ref.dtype)
        state_ref[...] = jnp.zeros(state_ref.shape, F32)

    lane = lax.broadcasted_iota(jnp.int32, (L, LANES), 1)
    low = lane < SSD_HEAD_DIM
    row = lax.broadcasted_iota(jnp.int32, (L, L), 0)
    col = lax.broadcasted_iota(jnp.int32, (L, L), 1)
    tril = row >= col
    tri = tril.astype(BF16)

    def per_head_to_lanes(v):
        v1 = v.astype(BF16)
        v2 = (v - v1.astype(F32)).astype(BF16)
        return _dot(v1, e_ref[...]) + _dot(v2, e_ref[...])

    for ci in range(xbc_ref.shape[0] // L):
        rs = slice(ci * L, (ci + 1) * L)
        x_in = xbc_ref[rs, :].astype(F32)
        ext = jnp.concatenate([tail_ref[...], x_in], axis=0)
        tail_ref[...] = x_in[L - SUBLANES:, :]
        acc = cb_ref[...] + cw_ref[SSD_CONV - 1:SSD_CONV, :] * x_in
        for d in range(1, SSD_CONV):
            acc = acc + cw_ref[SSD_CONV - 1 - d:SSD_CONV - d, :] * ext[SUBLANES - d:SUBLANES - d + L]
        xbc = acc * jax.nn.sigmoid(acc)
        xs = xbc[:, :SSD_INNER]
        bm = xbc[:, SSD_INNER:SSD_INNER + SSD_GROUPS * N]
        cm = xbc[:, SSD_INNER + SSD_GROUPS * N:]

        dt = jax.nn.softplus(dt_ref[rs, :] + dtb_ref[...])
        a = jnp.where(lane < SSD_HEADS, dt * (-jnp.exp(alog_ref[...])), 0.0)
        a1, a2, a3 = _split3(a)
        a_cs = _dot(tri, a1) + _dot(tri, a2) + _dot(tri, a3)
        a_cs_t = a_cs.T
        ea = jnp.exp(a_cs)
        ds = jnp.exp(a_cs[L - 1:L, :] - a_cs)
        dt_x = per_head_to_lanes(dt)
        dtds_x = per_head_to_lanes(dt * ds)
        ea_x = per_head_to_lanes(ea)

        y_tiles = []
        for g in range(SSD_GROUPS):
            bg = bm[:, g * N:(g + 1) * N]
            cg = cm[:, g * N:(g + 1) * N].astype(BF16)
            bg_t = bg.T.astype(BF16)
            cb = _dot(cg, bg_t)
            prev = state_ref[g]
            y_off = _dot(cg, prev.astype(BF16))
            xd_tiles = []
            for jj in range(2):
                j = 2 * g + jj
                ps = slice(j * LANES, (j + 1) * LANES)
                xs_p = xs[:, ps]
                xdt_p = xs_p * dt_x[:, ps]
                scores = []
                for h in (2 * j, 2 * j + 1):
                    diff = a_cs[:, h:h + 1] - a_cs_t[h:h + 1, :]
                    lmat = jnp.exp(jnp.where(tril, diff, -jnp.inf))
                    scores.append((cb * lmat).astype(BF16))
                lhs = jnp.concatenate(scores, axis=1)
                rhs = jnp.concatenate([jnp.where(low, xdt_p, 0.0), jnp.where(low, 0.0, xdt_p)],
                                      axis=0).astype(BF16)
                y_diag = _dot(lhs, rhs)
                y_tiles.append(y_diag + y_off[:, jj * LANES:(jj + 1) * LANES] * ea_x[:, ps]
                               + dskip_ref[:, ps] * xs_p)
                xd_tiles.append((xs_p * dtds_x[:, ps]).astype(BF16))
            xd = jnp.concatenate(xd_tiles, axis=1)
            decay = ea_x[L - 1:L, g * gw:(g + 1) * gw]
            state_ref[g] = prev * decay + _dot(bg_t, xd)

        y = jnp.concatenate(y_tiles, axis=1)
        z = z_ref[rs, :].astype(F32)
        y = y * (z * jax.nn.sigmoid(z))
        outs = []
        for g in range(SSD_GROUPS):
            yg = y[:, g * gw:(g + 1) * gw]
            outs.append(yg * lax.rsqrt(jnp.mean(yg * yg, axis=-1, keepdims=True) + EPS))
        o_ref[rs, :] = (jnp.concatenate(outs, axis=1) * ng_ref[...]).astype(o_ref.dtype)


def _ssd_mixer(u_wide, dt_raw, conv_w, conv_b, dt_bias, a_log, d_skip, norm_g, batch, seq):
    L = SSD_CHUNK
    T = SSD_STEP_CHUNKS * L
    ns = seq // T
    m = batch * seq

    def pad_heads(v):
        return jnp.pad(v.astype(F32), (0, LANES - SSD_HEADS)).reshape(1, LANES)

    dskip = jnp.repeat(d_skip.astype(F32), SSD_HEAD_DIM).reshape(1, SSD_INNER)
    e_np = np.zeros((LANES, SSD_INNER), np.float32)
    for hh in range(SSD_HEADS):
        e_np[hh, hh * SSD_HEAD_DIM:(hh + 1) * SSD_HEAD_DIM] = 1.0
    head_to_lanes = jnp.asarray(e_np, dtype=BF16)
    return pl.pallas_call(
        _ssd_kernel,
        grid=(batch, ns),
        in_specs=[
            pl.BlockSpec((T, SSD_XBC), lambda b, c: (b * ns + c, 0)),
            pl.BlockSpec((T, SSD_INNER), lambda b, c: (b * ns + c, 2 * SSD_XBC // SSD_INNER)),
            pl.BlockSpec((T, LANES), lambda b, c: (b * ns + c, 0)),
            pl.BlockSpec((SSD_CONV, SSD_XBC), lambda b, c: (0, 0)),
            pl.BlockSpec((1, SSD_XBC), lambda b, c: (0, 0)),
            pl.BlockSpec((1, LANES), lambda b, c: (0, 0)),
            pl.BlockSpec((1, LANES), lambda b, c: (0, 0)),
            pl.BlockSpec((1, SSD_INNER), lambda b, c: (0, 0)),
            pl.BlockSpec((1, SSD_INNER), lambda b, c: (0, 0)),
            pl.BlockSpec((LANES, SSD_INNER), lambda b, c: (0, 0)),
        ],
        out_specs=pl.BlockSpec((T, SSD_INNER), lambda b, c: (b * ns + c, 0)),
        out_shape=jax.ShapeDtypeStruct((m, SSD_INNER), ACT_DTYPE),
        scratch_shapes=[pltpu.VMEM((SUBLANES, SSD_XBC), F32),
                        pltpu.VMEM((SSD_GROUPS, SSD_STATE, SSD_INNER // SSD_GROUPS), F32)],
        compiler_params=_cparams("parallel", "arbitrary"),
        name="ssd_mixer",
    )(u_wide, u_wide, dt_raw, conv_w.astype(F32), conv_b.astype(F32).reshape(1, SSD_XBC),
      pad_heads(dt_bias), pad_heads(a_log), dskip, norm_g.astype(F32).reshape(1, SSD_INNER), head_to_lanes)


def _conf_kernel(ag_ref, w_ref, b_ref, g_ref, beta_ref, o_ref, sh_ref, y_ref):
    T = CONF_ROWS
    H = CONF_HALO
    strips = CONF_DIM // LANES

    @pl.when(pl.program_id(1) == 0)
    def _():
        sh_ref[...] = jnp.zeros(sh_ref.shape, F32)

    a = ag_ref[:, :CONF_DIM].astype(F32)
    gate = ag_ref[:, CONF_DIM:].astype(F32)
    h = a * jax.nn.sigmoid(gate)
    for s in range(SUBLANES):
        for c in range(strips):
            sh_ref[s, c, 0:H, :] = sh_ref[s, c, T:T + H, :]
            sh_ref[s, c, pl.ds(H - s, T), :] = h[:, c * LANES:(c + 1) * LANES]

    base = H - (CONF_KERNEL - 1)
    nt = 16
    groups = T // (nt * SUBLANES)

    def strip_group(gi, carry):
        c = gi // groups
        r0 = pl.multiple_of((gi % groups) * (nt * SUBLANES), nt * SUBLANES)
        tiles = [jnp.broadcast_to(b_ref[c], (SUBLANES, LANES))] * nt
        for j in range(CONF_KERNEL):
            q, s = divmod(base + j, SUBLANES)
            wj = jnp.broadcast_to(w_ref[c, j:j + 1, :], (SUBLANES, LANES))
            tiles = [t + wj * sh_ref[s, c, pl.ds(r0 + (q + i) * SUBLANES, SUBLANES), :]
                     for i, t in enumerate(tiles)]
        y_ref[c, pl.ds(r0, nt * SUBLANES), :] = jnp.concatenate(tiles, axis=0)
        return carry

    lax.fori_loop(0, strips * groups, strip_group, 0)

    ys = [y_ref[c] for c in range(strips)]
    mu = jnp.sum(sum(ys), axis=-1, keepdims=True) * (1.0 / CONF_DIM)
    cen = [v - mu for v in ys]
    var = jnp.sum(sum(v * v for v in cen), axis=-1, keepdims=True) * (1.0 / CONF_DIM)
    inv = lax.rsqrt(var + EPS)
    for c in range(strips):
        cs = slice(c * LANES, (c + 1) * LANES)
        hn = cen[c] * inv * g_ref[:, cs] + beta_ref[:, cs]
        o_ref[:, cs] = (hn * jax.nn.sigmoid(hn)).astype(o_ref.dtype)


def _conf_mixer(u_wide, conv_w, conv_b, ln_g, ln_b, batch, seq):
    T = CONF_ROWS
    nt = seq // T
    m = batch * seq
    vec = lambda v: v.astype(F32).reshape(1, CONF_DIM)
    strips = CONF_DIM // LANES
    taps_pad = CONF_KERNEL + 1
    w_strips = jnp.pad(conv_w.astype(F32), ((0, 1), (0, 0))).reshape(taps_pad, strips, LANES).transpose(1, 0, 2)
    b_strips = conv_b.astype(F32).reshape(strips, 1, LANES)
    return pl.pallas_call(
        _conf_kernel,
        grid=(batch, nt),
        in_specs=[
            pl.BlockSpec((T, 2 * CONF_DIM), lambda b, c: (b * nt + c, 1)),
            pl.BlockSpec((strips, taps_pad, LANES), lambda b, c: (0, 0, 0)),
            pl.BlockSpec((strips, 1, LANES), lambda b, c: (0, 0, 0)),
            pl.BlockSpec((1, CONF_DIM), lambda b, c: (0, 0)),
            pl.BlockSpec((1, CONF_DIM), lambda b, c: (0, 0)),
        ],
        out_specs=pl.BlockSpec((T, CONF_DIM), lambda b, c: (b * nt + c, 0)),
        out_shape=jax.ShapeDtypeStruct((m, CONF_DIM), ACT_DTYPE),
        scratch_shapes=[pltpu.VMEM((SUBLANES, CONF_DIM // LANES, T + CONF_HALO, LANES), F32),
                        pltpu.VMEM((CONF_DIM // LANES, T, LANES), F32)],
        compiler_params=_cparams("parallel", "arbitrary"),
        name="conformer_mixer",
    )(u_wide, w_strips, b_strips, vec(ln_g), vec(ln_b))


def _mix_ffn_kernel(x_ref, a_ref, b_ref, wa_ref, wb_ref, g_ref, wg_ref, wu_ref, wd_ref, fg_ref, o_ref, act_ref,
                    *, final_norm):
    x = x_ref[...] + _dot(a_ref[...], wa_ref[...]) + _dot(b_ref[...], wb_ref[...])
    h = _rms(x, g_ref[...]).astype(BF16)
    for k in range(wg_ref.shape[1] // FFN_COLS):
        sl = slice(k * FFN_COLS, (k + 1) * FFN_COLS)
        gate = _dot(h, wg_ref[:, sl])
        up = _dot(h, wu_ref[:, sl])
        act_ref[:, sl] = (gate * jax.nn.sigmoid(gate) * up).astype(BF16)
    y = x + _dot(act_ref[...], wd_ref[...])
    if final_norm:
        y = _rms(y, fg_ref[...])
    o_ref[...] = y


def _mix_ffn(x, a, b, w_out, g, w_gate, w_up, w_down, final_g, name):
    m, d = x.shape
    ka, kb = a.shape[1], b.shape[1]
    f = w_gate.shape[1]
    tm = FFN_ROWS
    final_norm = final_g is not None
    fg = (final_g if final_norm else jnp.ones((d,), F32)).astype(F32).reshape(1, d)
    resident = dict(pipeline_mode=pl.Buffered(1))
    return pl.pallas_call(
        functools.partial(_mix_ffn_kernel, final_norm=final_norm),
        grid=(m // tm,),
        in_specs=[pl.BlockSpec((tm, d), lambda i: (i, 0)),
                  pl.BlockSpec((tm, ka), lambda i: (i, 0)),
                  pl.BlockSpec((tm, kb), lambda i: (i, 0)),
                  pl.BlockSpec((ka, d), lambda i: (0, 0), **resident),
                  pl.BlockSpec((kb, d), lambda i: (0, 0), **resident),
                  pl.BlockSpec((1, d), lambda i: (0, 0)),
                  pl.BlockSpec((d, f), lambda i: (0, 0), **resident),
                  pl.BlockSpec((d, f), lambda i: (0, 0), **resident),
                  pl.BlockSpec((f, d), lambda i: (0, 0), **resident),
                  pl.BlockSpec((1, d), lambda i: (0, 0))],
        out_specs=pl.BlockSpec((tm, d), lambda i: (i, 0)),
        out_shape=jax.ShapeDtypeStruct((m, d), F32),
        scratch_shapes=[pltpu.VMEM((tm, f), BF16)],
        compiler_params=_cparams("parallel"),
        name=name,
    )(x, a, b, w_out[:ka].astype(BF16), w_out[ka:].astype(BF16), g.astype(F32).reshape(1, d),
      w_gate.astype(BF16), w_up.astype(BF16), w_down.astype(BF16), fg)


def _s5_kernel(u_ref, bre_ref, bim_ref, cre_ref, cim_ref, lre_ref, lim_ref, d_ref,
               gw_ref, gb_ref, o_ref, sre_ref, sim_ref, st_ref, rl_ref):
    nb, ts, dim = u_ref.shape
    rows = nb * ts
    half_in = dim // 2

    @pl.when(pl.program_id(0) == 0)
    def _():
        st_ref[...] = jnp.zeros(st_ref.shape, F32)

    for b in range(nb):
        ub = u_ref[b].astype(F32)
        for k in range(dim // LANES):
            rl_ref[k, pl.ds(b, ts, stride=nb), :] = ub[:, k * LANES:(k + 1) * LANES]
    u_tm = jnp.concatenate([rl_ref[k] for k in range(dim // LANES)], axis=1)
    u16 = u_tm.astype(BF16)

    for hf in range(2):
        uh = u16[:, hf * half_in:(hf + 1) * half_in]
        sre_ref[:, hf * S5_HALF:(hf + 1) * S5_HALF] = _dot(uh, bre_ref[hf])
        sim_ref[:, hf * S5_HALF:(hf + 1) * S5_HALF] = _dot(uh, bim_ref[hf])

    for hf in range(2):
        sl = slice(hf * S5_HALF, (hf + 1) * S5_HALF)
        lr = lre_ref[:, sl]
        li = lim_ref[:, sl]

        def step(t, carry, sl=sl, lr=lr, li=li):
            sr, si = carry
            r0 = pl.multiple_of(t * nb, nb)
            nr = lr * sr - li * si + sre_ref[pl.ds(r0, nb), sl]
            ni = lr * si + li * sr + sim_ref[pl.ds(r0, nb), sl]
            sre_ref[pl.ds(r0, nb), sl] = nr
            sim_ref[pl.ds(r0, nb), sl] = ni
            return nr, ni

        sr, si = lax.fori_loop(0, ts, step, (st_ref[0, :, sl], st_ref[1, :, sl]), unroll=True)
        st_ref[0, :, sl] = sr
        st_ref[1, :, sl] = si

    ys = []
    for hf in range(2):
        sl = slice(hf * S5_HALF, (hf + 1) * S5_HALF)
        ys.append(_dot(sre_ref[:, sl].astype(BF16), cre_ref[hf]) + _dot(sim_ref[:, sl].astype(BF16), cim_ref[hf]))
    y = jnp.concatenate(ys, axis=1) + d_ref[...] * u_tm
    y = jax.nn.gelu(y)
    y = y * jax.nn.sigmoid(_dot(y.astype(BF16), gw_ref[...]) + gb_ref[...])
    for k in range(dim // LANES):
        rl_ref[k] = y[:, k * LANES:(k + 1) * LANES]
    for b in range(nb):
        for k in range(dim // LANES):
            o_ref[b, :, k * LANES:(k + 1) * LANES] = rl_ref[k, pl.ds(b, ts, stride=nb), :].astype(o_ref.dtype)


def _s5_mixer(u_wide3, lam_re, lam_im, b_re, b_im, c_re, c_im, log_dt, d_skip, glu_w, glu_b, batch, seq):
    ts = S5_STEPS
    rows = batch * ts
    G, P, C = S5_GROUPS, S5_STATE, S5_GROUP
    hg = G // 2
    lam = lax.complex(lam_re.astype(F32), lam_im.astype(F32))
    dt = jnp.exp(log_dt.astype(F32))[:, None]
    lam_bar = jnp.exp(lam * dt)
    b_bar = ((lam_bar - 1.0) / lam)[..., None] * lax.complex(b_re.astype(F32), b_im.astype(F32))
    eye = jnp.eye(hg, dtype=F32)

    def blockdiag_in(v):
        blk = v.transpose(0, 2, 1).reshape(2, hg, C, P)
        return jnp.einsum('hgcp,gk->hgckp', blk, eye).reshape(2, hg * C, hg * P).astype(BF16)

    def blockdiag_out(v):
        blk = v.transpose(0, 2, 1).reshape(2, hg, P, C)
        return jnp.einsum('hgpc,gk->hgpkc', blk, eye).reshape(2, hg * P, hg * C).astype(BF16)

    bre = blockdiag_in(jnp.real(b_bar))
    bim = blockdiag_in(jnp.imag(b_bar))
    cre = blockdiag_out(c_re.astype(F32))
    cim = blockdiag_out(-c_im.astype(F32))
    lre = jnp.broadcast_to(jnp.real(lam_bar).reshape(1, G * P), (batch, G * P))
    lim = jnp.broadcast_to(jnp.imag(lam_bar).reshape(1, G * P), (batch, G * P))
    ublk = u_wide3.shape[2] // S5_DIM - 1
    const2 = lambda i: (0, 0)
    const3 = lambda i: (0, 0, 0)
    return pl.pallas_call(
        _s5_kernel,
        grid=(seq // ts,),
        in_specs=[
            pl.BlockSpec((batch, ts, S5_DIM), lambda i: (0, i, ublk)),
            pl.BlockSpec((2, hg * C, hg * P), const3),
            pl.BlockSpec((2, hg * C, hg * P), const3),
            pl.BlockSpec((2, hg * P, hg * C), const3),
            pl.BlockSpec((2, hg * P, hg * C), const3),
            pl.BlockSpec((batch, G * P), const2),
            pl.BlockSpec((batch, G * P), const2),
            pl.BlockSpec((1, S5_DIM), const2),
            pl.BlockSpec((S5_DIM, S5_DIM), const2),
            pl.BlockSpec((1, S5_DIM), const2),
        ],
        out_specs=pl.BlockSpec((batch, ts, S5_DIM), lambda i: (0, i, 0)),
        out_shape=jax.ShapeDtypeStruct((batch, seq, S5_DIM), ACT_DTYPE),
        scratch_shapes=[pltpu.VMEM((rows, S5_LANES), F32), pltpu.VMEM((rows, S5_LANES), F32),
                        pltpu.VMEM((2, batch, S5_LANES), F32),
                        pltpu.VMEM((S5_DIM // LANES, rows, LANES), F32)],
        compiler_params=_cparams("arbitrary"),
        name="s5_mixer",
    )(u_wide3, bre, bim, cre, cim, lre, lim, d_skip.astype(F32).reshape(1, S5_DIM),
      glu_w.astype(BF16), glu_b.astype(F32).reshape(1, S5_DIM))


def _ret_kernel(q_ref, k_ref, v_ref, gate_ref, cos_ref, sin_ref, dmat_ref, qdec_ref, kdec_ref, cg_ref,
                gng_ref, gnb_ref, o_ref, state_ref):
    L = RET_CHUNK
    dk, dv = RET_KEY_DIM, RET_VAL_DIM
    half = dk // 2

    @pl.when(pl.program_id(1) == 0)
    def _():
        state_ref[...] = jnp.zeros(state_ref.shape, F32)

    def rot(t, cos, sin):
        t1, t2 = t[:, :half], t[:, half:]
        return jnp.concatenate([t1 * cos - t2 * sin, t1 * sin + t2 * cos], axis=1)

    for ci in range(q_ref.shape[0] // L):
        rs = slice(ci * L, (ci + 1) * L)
        cos = cos_ref[rs, :]
        sin = sin_ref[rs, :]
        for h in range(RET_HEADS):
            q = rot(q_ref[rs, h * dk:(h + 1) * dk].astype(F32), cos, sin)
            k = rot(k_ref[rs, h * dk:(h + 1) * dk].astype(F32), cos, sin) * (dk ** -0.5)
            v = v_ref[rs, h * dv:(h + 1) * dv]
            q16 = q.astype(BF16)
            k_t = k.T
            scores = _dot(q16, k_t.astype(BF16)) * dmat_ref[h]
            prev = state_ref[h]
            o = _dot(scores.astype(BF16), v) + _dot(q16, prev.astype(BF16)) * qdec_ref[:, h:h + 1]
            kv = _dot((k_t * kdec_ref[h:h + 1, :]).astype(BF16), v)
            state_ref[h] = prev * cg_ref[h:h + 1, :] + kv
            mu = jnp.mean(o, axis=-1, keepdims=True)
            var = jnp.mean(jnp.square(o - mu), axis=-1, keepdims=True)
            on = ((o - mu) * lax.rsqrt(var + EPS) * gng_ref[:, h * dv:(h + 1) * dv]
                  + gnb_ref[:, h * dv:(h + 1) * dv])
            gt = gate_ref[rs, h * dv:(h + 1) * dv].astype(F32)
            o_ref[rs, h * dv:(h + 1) * dv] = (gt * jax.nn.sigmoid(gt) * on).astype(o_ref.dtype)


def _ret_mixer(u_wide, gn_g, gn_b, batch, seq):
    L = RET_CHUNK
    m = batch * seq
    inv_freq = ROPE_BASE ** (-jnp.arange(0, RET_KEY_DIM, 2, dtype=F32) / RET_KEY_DIM)
    ang = jnp.arange(seq, dtype=F32)[:, None] * inv_freq[None, :]
    cos, sin = jnp.cos(ang), jnp.sin(ang)
    gamma = 1.0 - jnp.exp(jnp.linspace(math.log(1.0 / 32), math.log(1.0 / 512), RET_HEADS, dtype=F32))
    log_g = jnp.log(gamma)
    idx = jnp.arange(L, dtype=F32)
    diff = idx[:, None] - idx[None, :]
    dmat = jnp.where(diff >= 0, jnp.exp(log_g[:, None, None] * jnp.maximum(diff, 0.0)), 0.0)
    kdec = jnp.exp(log_g[:, None] * (L - 1.0 - idx)[None, :])
    kdec = jnp.pad(kdec, ((0, SUBLANES - RET_HEADS), (0, 0)))
    qdec = jnp.exp(log_g[None, :] * (idx + 1.0)[:, None])
    qdec = jnp.pad(qdec, ((0, 0), (0, LANES - RET_HEADS)))
    cg = jnp.broadcast_to(jnp.exp(log_g * L)[:, None], (RET_HEADS, RET_VAL_DIM))
    cg = jnp.pad(cg, ((0, SUBLANES - RET_HEADS), (0, 0)))
    T = RET_STEP_CHUNKS * L
    ns = seq // T
    row = lambda b, c: b * ns + c
    return pl.pallas_call(
        _ret_kernel,
        grid=(batch, ns),
        in_specs=[
            pl.BlockSpec((T, RET_QK), lambda b, c: (row(b, c), 0)),
            pl.BlockSpec((T, RET_QK), lambda b, c: (row(b, c), 1)),
            pl.BlockSpec((T, RET_V), lambda b, c: (row(b, c), 2)),
            pl.BlockSpec((T, RET_V), lambda b, c: (row(b, c), 3)),
            pl.BlockSpec((T, RET_KEY_DIM // 2), lambda b, c: (c, 0)),
            pl.BlockSpec((T, RET_KEY_DIM // 2), lambda b, c: (c, 0)),
            pl.BlockSpec((RET_HEADS, L, L), lambda b, c: (0, 0, 0)),
            pl.BlockSpec((L, LANES), lambda b, c: (0, 0)),
            pl.BlockSpec((SUBLANES, L), lambda b, c: (0, 0)),
            pl.BlockSpec((SUBLANES, RET_VAL_DIM), lambda b, c: (0, 0)),
            pl.BlockSpec((1, RET_V), lambda b, c: (0, 0)),
            pl.BlockSpec((1, RET_V), lambda b, c: (0, 0)),
        ],
        out_specs=pl.BlockSpec((T, RET_V), lambda b, c: (row(b, c), 0)),
        out_shape=jax.ShapeDtypeStruct((m, RET_V), ACT_DTYPE),
        scratch_shapes=[pltpu.VMEM((RET_HEADS, RET_KEY_DIM, RET_VAL_DIM), F32)],
        compiler_params=_cparams("parallel", "arbitrary"),
        name="retention_mixer",
    )(u_wide, u_wide, u_wide, u_wide, cos, sin, dmat, qdec, kdec, cg,
      gn_g.astype(F32).reshape(1, RET_V), gn_b.astype(F32).reshape(1, RET_V))


def kernel(x, ev_norm, ev_w_in, ev_conv_w, ev_conv_b, ev_dt_bias, ev_a_log, ev_d, ev_ssd_norm, ev_cf_conv_w, ev_cf_conv_b, ev_cf_ln_g, ev_cf_ln_b, ev_w_out, od_norm, od_w_in, od_lam_re, od_lam_im, od_b_re, od_b_im, od_c_re, od_c_im, od_log_dt, od_s5_d, od_glu_w, od_glu_b, od_gn_g, od_gn_b, od_w_out, ffn_norm, ffn_w_gate, ffn_w_up, ffn_w_down, final_norm):
    batch, seq, d = x.shape
    m = batch * seq
    xf = x.reshape(m, d)

    w = ev_w_in[0]
    o1 = SSD_INNER
    o2 = o1 + SSD_XBC
    o3 = o2 + SSD_HEADS
    w_main = jnp.concatenate([w[:, o1:o2], w[:, o3:], w[:, :o1]], axis=1).astype(BF16)
    w_dt = jnp.pad(w[:, o2:o3], ((0, 0), (0, LANES - SSD_HEADS))).astype(BF16)
    u_wide, dt_raw = _norm_proj(xf, ev_norm[0], w_main, w_dt, name="even_in_proj")
    ya = _ssd_mixer(u_wide, dt_raw, ev_conv_w[0], ev_conv_b[0], ev_dt_bias[0], ev_a_log[0], ev_d[0],
                    ev_ssd_norm[0], batch, seq)
    yb = _conf_mixer(u_wide, ev_cf_conv_w[0], ev_cf_conv_b[0], ev_cf_ln_g[0], ev_cf_ln_b[0], batch, seq)
    xf = _mix_ffn(xf, ya, yb, ev_w_out[0], ffn_norm[0], ffn_w_gate[0], ffn_w_up[0], ffn_w_down[0], None,
                  "even_out_ffn")

    w = od_w_in[0]
    w_main = jnp.concatenate([w[:, S5_DIM:], w[:, :S5_DIM]], axis=1).astype(BF16)
    (u_wide,) = _norm_proj(xf, od_norm[0], w_main, None, name="odd_in_proj")
    yc = _s5_mixer(u_wide.reshape(batch, seq, -1), od_lam_re[0], od_lam_im[0], od_b_re[0], od_b_im[0],
                   od_c_re[0], od_c_im[0], od_log_dt[0], od_s5_d[0], od_glu_w[0], od_glu_b[0], batch, seq)
    yd = _ret_mixer(u_wide, od_gn_g[0], od_gn_b[0], batch, seq)
    xf = _mix_ffn(xf, yc.reshape(m, S5_DIM), yd, od_w_out[0], ffn_norm[1], ffn_w_gate[1], ffn_w_up[1],
                  ffn_w_down[1], final_norm, "odd_out_ffn")
    return xf.reshape(batch, seq, d)
```

```python
import functools
import math

import numpy as np
import jax
import jax.numpy as jnp
from jax import lax
from jax.experimental import pallas as pl
from jax.experimental.pallas import tpu as pltpu

F32 = jnp.float32
BF16 = jnp.bfloat16
ACT_DTYPE = BF16

D_MODEL = 1024
EPS = 1e-5

SSD_HEADS = 16
SSD_HEAD_DIM = 64
SSD_INNER = SSD_HEADS * SSD_HEAD_DIM
SSD_GROUPS = 4
SSD_STATE = 128
SSD_CONV = 4
SSD_CHUNK = 128
SSD_XBC = SSD_INNER + 2 * SSD_GROUPS * SSD_STATE

CONF_DIM = 1024
CONF_KERNEL = 31

S5_DIM = 512
S5_GROUP = 16
S5_GROUPS = S5_DIM // S5_GROUP
S5_STATE = 64
S5_LANES = S5_GROUPS * S5_STATE
S5_HALF = S5_LANES // 2

RET_HEADS = 4
RET_KEY_DIM = 256
RET_VAL_DIM = 256
RET_CHUNK = 128
RET_QK = RET_HEADS * RET_KEY_DIM
RET_V = RET_HEADS * RET_VAL_DIM
ROPE_BASE = 10000.0

LANES = 128
SUBLANES = 8
VMEM_LIMIT_BYTES = 56 * 1024 * 1024

PROJ_ROWS = 512
PROJ_COLS = 512
FFN_ROWS = 512
FFN_COLS = 256
CONF_ROWS = 256
CONF_HALO = 32
S5_STEPS = 128
RET_STEP_CHUNKS = 2
SSD_STEP_CHUNKS = 4


def _cparams(*sem):
    return pltpu.CompilerParams(dimension_semantics=sem, vmem_limit_bytes=VMEM_LIMIT_BYTES)


def _dot(a, b):
    return jnp.dot(a, b, preferred_element_type=F32)


def _split3(x):
    x1 = x.astype(BF16)
    r1 = x - x1.astype(F32)
    x2 = r1.astype(BF16)
    x3 = (r1 - x2.astype(F32)).astype(BF16)
    return x1, x2, x3


def _rms(x, g):
    return x * lax.rsqrt(jnp.mean(x * x, axis=-1, keepdims=True) + EPS) * g


def _norm_proj_kernel(*refs, has_aux):
    if has_aux:
        x_ref, g_ref, w_ref, waux_ref, o_ref, oaux_ref = refs
    else:
        x_ref, g_ref, w_ref, o_ref = refs
    h = _rms(x_ref[...], g_ref[...]).astype(BF16)
    for k in range(w_ref.shape[1] // PROJ_COLS):
        sl = slice(k * PROJ_COLS, (k + 1) * PROJ_COLS)
        o_ref[:, sl] = _dot(h, w_ref[:, sl]).astype(o_ref.dtype)
    if has_aux:
        oaux_ref[...] = _dot(h, waux_ref[...])


def _norm_proj(x, g, w, w_aux, *, name):
    m, d = x.shape
    n = w.shape[1]
    tm = PROJ_ROWS
    has_aux = w_aux is not None
    resident = dict(pipeline_mode=pl.Buffered(1))
    in_specs = [pl.BlockSpec((tm, d), lambda i: (i, 0)),
                pl.BlockSpec((1, d), lambda i: (0, 0)),
                pl.BlockSpec((d, n), lambda i: (0, 0), **resident)]
    out_specs = [pl.BlockSpec((tm, n), lambda i: (i, 0))]
    out_shape = [jax.ShapeDtypeStruct((m, n), ACT_DTYPE)]
    args = [x, g.reshape(1, d), w]
    if has_aux:
        in_specs.append(pl.BlockSpec((d, LANES), lambda i: (0, 0), **resident))
        out_specs.append(pl.BlockSpec((tm, LANES), lambda i: (i, 0)))
        out_shape.append(jax.ShapeDtypeStruct((m, LANES), F32))
        args.append(w_aux)
    return pl.pallas_call(
        functools.partial(_norm_proj_kernel, has_aux=has_aux),
        grid=(m // tm,),
        in_specs=in_specs, out_specs=out_specs, out_shape=out_shape,
        compiler_params=_cparams("parallel"),
        name=name,
    )(*args)


def _ssd_kernel(xbc_ref, dt_ref, cw_ref, cb_ref, dtb_ref, alog_ref, dskip_ref, e_ref, o_ref, tail_ref, state_ref):
    L = SSD_CHUNK
    N = SSD_STATE
    gw = SSD_INNER // SSD_GROUPS

    @pl.when(pl.program_id(1) == 0)
    def _():
        tail_ref[...] = jnp.zeros(tail_ref.shape, tail_ref.dtype)
        state_ref[...] = jnp.zeros(state_ref.shape, F32)

    lane = lax.broadcasted_iota(jnp.int32, (L, LANES), 1)
    low = lane < SSD_HEAD_DIM
    row = lax.broadcasted_iota(jnp.int32, (L, L), 0)
    col = lax.broadcasted_iota(jnp.int32, (L, L), 1)
    tril = row >= col
    tri = tril.astype(BF16)

    def per_head_to_lanes(v):
        v1 = v.astype(BF16)
        v2 = (v - v1.astype(F32)).astype(BF16)
        return _dot(v1, e_ref[...]) + _dot(v2, e_ref[...])

    for ci in range(xbc_ref.shape[0] // L):
        rs = slice(ci * L, (ci + 1) * L)
        x_in = xbc_ref[rs, :].astype(F32)
        ext = jnp.concatenate([tail_ref[...], x_in], axis=0)
        tail_ref[...] = x_in[L - SUBLANES:, :]
        acc = cb_ref[...] + cw_ref[SSD_CONV - 1:SSD_CONV, :] * x_in
        for d in range(1, SSD_CONV):
            acc = acc + cw_ref[SSD_CONV - 1 - d:SSD_CONV - d, :] * ext[SUBLANES - d:SUBLANES - d + L]
        xbc = acc * jax.nn.sigmoid(acc)
        xs = xbc[:, :SSD_INNER]
        bm = xbc[:, SSD_INNER:SSD_INNER + SSD_GROUPS * N]
        cm = xbc[:, SSD_INNER + SSD_GROUPS * N:]

        dt = jax.nn.softplus(dt_ref[rs, :] + dtb_ref[...])
        a = jnp.where(lane < SSD_HEADS, dt * (-jnp.exp(alog_ref[...])), 0.0)
        a1, a2, a3 = _split3(a)
        a_cs = _dot(tri, a1) + _dot(tri, a2) + _dot(tri, a3)
        a_cs_t = a_cs.T
        ea = jnp.exp(a_cs)
        ds = jnp.exp(a_cs[L - 1:L, :] - a_cs)
        dt_x = per_head_to_lanes(dt)
        dtds_x = per_head_to_lanes(dt * ds)
        ea_x = per_head_to_lanes(ea)

        y_tiles = []
        for g in range(SSD_GROUPS):
            bg = bm[:, g * N:(g + 1) * N]
            cg = cm[:, g * N:(g + 1) * N].astype(BF16)
            bg_t = bg.T.astype(BF16)
            cb = _dot(cg, bg_t)
            prev = state_ref[g]
            y_off = _dot(cg, prev.astype(BF16))
            xd_tiles = []
            for jj in range(2):
                j = 2 * g + jj
                ps = slice(j * LANES, (j + 1) * LANES)
                xs_p = xs[:, ps]
                xdt_p = xs_p * dt_x[:, ps]
                scores = []
                for h in (2 * j, 2 * j + 1):
                    diff = a_cs[:, h:h + 1] - a_cs_t[h:h + 1, :]
                    lmat = jnp.exp(jnp.where(tril, diff, -jnp.inf))
                    scores.append((cb * lmat).astype(BF16))
                lhs = jnp.concatenate(scores, axis=1)
                rhs = jnp.concatenate([jnp.where(low, xdt_p, 0.0), jnp.where(low, 0.0, xdt_p)],
                                      axis=0).astype(BF16)
                y_diag = _dot(lhs, rhs)
                y_tiles.append(y_diag + y_off[:, jj * LANES:(jj + 1) * LANES] * ea_x[:, ps]
                               + dskip_ref[:, ps] * xs_p)
                xd_tiles.append((xs_p * dtds_x[:, ps]).astype(BF16))
            xd = jnp.concatenate(xd_tiles, axis=1)
            decay = ea_x[L - 1:L, g * gw:(g + 1) * gw]
            state_ref[g] = prev * decay + _dot(bg_t, xd)

        o_ref[rs, :] = jnp.concatenate(y_tiles, axis=1).astype(o_ref.dtype)


def _ssd_mixer(u_wide, dt_raw, conv_w, conv_b, dt_bias, a_log, d_skip, batch, seq):
    L = SSD_CHUNK
    T = SSD_STEP_CHUNKS * L
    ns = seq // T
    m = batch * seq

    def pad_heads(v):
        return jnp.pad(v.astype(F32), (0, LANES - SSD_HEADS)).reshape(1, LANES)

    dskip = jnp.repeat(d_skip.astype(F32), SSD_HEAD_DIM).reshape(1, SSD_INNER)
    e_np = np.zeros((LANES, SSD_INNER), np.float32)
    for hh in range(SSD_HEADS):
        e_np[hh, hh * SSD_HEAD_DIM:(hh + 1) * SSD_HEAD_DIM] = 1.0
    head_to_lanes = jnp.asarray(e_np, dtype=BF16)
    return pl.pallas_call(
        _ssd_kernel,
        grid=(batch, ns),
        in_specs=[
            pl.BlockSpec((T, SSD_XBC), lambda b, c: (b * ns + c, 0)),
            pl.BlockSpec((T, LANES), lambda b, c: (b * ns + c, 0)),
            pl.BlockSpec((SSD_CONV, SSD_XBC), lambda b, c: (0, 0)),
            pl.BlockSpec((1, SSD_XBC), lambda b, c: (0, 0)),
            pl.BlockSpec((1, LANES), lambda b, c: (0, 0)),
            pl.BlockSpec((1, LANES), lambda b, c: (0, 0)),
            pl.BlockSpec((1, SSD_INNER), lambda b, c: (0, 0)),
            pl.BlockSpec((LANES, SSD_INNER), lambda b, c: (0, 0)),
        ],
        out_specs=pl.BlockSpec((T, SSD_INNER), lambda b, c: (b * ns + c, 0)),
        out_shape=jax.ShapeDtypeStruct((m, SSD_INNER), ACT_DTYPE),
        scratch_shapes=[pltpu.VMEM((SUBLANES, SSD_XBC), F32),
                        pltpu.VMEM((SSD_GROUPS, SSD_STATE, SSD_INNER // SSD_GROUPS), F32)],
        compiler_params=_cparams("parallel", "arbitrary"),
        name="ssd_mixer",
    )(u_wide, dt_raw, conv_w.astype(F32), conv_b.astype(F32).reshape(1, SSD_XBC),
      pad_heads(dt_bias), pad_heads(a_log), dskip, head_to_lanes)


def _conf_kernel(ag_ref, w_ref, b_ref, o_ref, sh_ref, y_ref):
    T = CONF_ROWS
    H = CONF_HALO
    strips = CONF_DIM // LANES

    @pl.when(pl.program_id(1) == 0)
    def _():
        sh_ref[...] = jnp.zeros(sh_ref.shape, F32)

    a = ag_ref[:, :CONF_DIM].astype(F32)
    gate = ag_ref[:, CONF_DIM:].astype(F32)
    h = a * jax.nn.sigmoid(gate)
    for s in range(SUBLANES):
        for c in range(strips):
            sh_ref[s, c, 0:H, :] = sh_ref[s, c, T:T + H, :]
            sh_ref[s, c, pl.ds(H - s, T), :] = h[:, c * LANES:(c + 1) * LANES]

    base = H - (CONF_KERNEL - 1)
    nt = 16
    groups = T // (nt * SUBLANES)

    def strip_group(gi, carry):
        c = gi // groups
        r0 = pl.multiple_of((gi % groups) * (nt * SUBLANES), nt * SUBLANES)
        tiles = [jnp.broadcast_to(b_ref[c], (SUBLANES, LANES))] * nt
        for j in range(CONF_KERNEL):
            q, s = divmod(base + j, SUBLANES)
            wj = jnp.broadcast_to(w_ref[c, j:j + 1, :], (SUBLANES, LANES))
            tiles = [t + wj * sh_ref[s, c, pl.ds(r0 + (q + i) * SUBLANES, SUBLANES), :]
                     for i, t in enumerate(tiles)]
        y_ref[c, pl.ds(r0, nt * SUBLANES), :] = jnp.concatenate(tiles, axis=0)
        return carry

    lax.fori_loop(0, strips * groups, strip_group, 0)

    for c in range(strips):
        o_ref[:, c * LANES:(c + 1) * LANES] = y_ref[c].astype(o_ref.dtype)


def _conf_mixer(u_wide, conv_w, conv_b, batch, seq):
    T = CONF_ROWS
    nt = seq // T
    m = batch * seq
    strips = CONF_DIM // LANES
    taps_pad = CONF_KERNEL + 1
    w_strips = jnp.pad(conv_w.astype(F32), ((0, 1), (0, 0))).reshape(taps_pad, strips, LANES).transpose(1, 0, 2)
    b_strips = conv_b.astype(F32).reshape(strips, 1, LANES)
    return pl.pallas_call(
        _conf_kernel,
        grid=(batch, nt),
        in_specs=[
            pl.BlockSpec((T, 2 * CONF_DIM), lambda b, c: (b * nt + c, 1)),
            pl.BlockSpec((strips, taps_pad, LANES), lambda b, c: (0, 0, 0)),
            pl.BlockSpec((strips, 1, LANES), lambda b, c: (0, 0, 0)),
        ],
        out_specs=pl.BlockSpec((T, CONF_DIM), lambda b, c: (b * nt + c, 0)),
        out_shape=jax.ShapeDtypeStruct((m, CONF_DIM), ACT_DTYPE),
        scratch_shapes=[pltpu.VMEM((SUBLANES, CONF_DIM // LANES, T + CONF_HALO, LANES), F32),
                        pltpu.VMEM((CONF_DIM // LANES, T, LANES), F32)],
        compiler_params=_cparams("parallel", "arbitrary"),
        name="conformer_mixer",
    )(u_wide, w_strips, b_strips)


def _finish_mixer(v_ref, aux_ref, gain_ref, bias_ref, kind):
    if kind is None:
        return v_ref[...]
    v = v_ref[...].astype(F32)
    width = 2 * LANES
    if kind == "ln_swish":
        mu = jnp.mean(v, axis=-1, keepdims=True)
        var = jnp.mean(jnp.square(v - mu), axis=-1, keepdims=True)
        hn = (v - mu) * lax.rsqrt(var + EPS) * gain_ref[...] + bias_ref[...]
        return (hn * jax.nn.sigmoid(hn)).astype(BF16)
    aux = aux_ref[...].astype(F32)
    gate = aux * jax.nn.sigmoid(aux)
    outs = []
    for gi in range(v.shape[1] // width):
        sl = slice(gi * width, (gi + 1) * width)
        if kind == "gate_rms":
            vg = v[:, sl] * gate[:, sl]
            outs.append(vg * lax.rsqrt(jnp.mean(vg * vg, axis=-1, keepdims=True) + EPS) * gain_ref[:, sl])
        else:
            vg = v[:, sl]
            mu = jnp.mean(vg, axis=-1, keepdims=True)
            var = jnp.mean(jnp.square(vg - mu), axis=-1, keepdims=True)
            on = (vg - mu) * lax.rsqrt(var + EPS) * gain_ref[:, sl] + bias_ref[:, sl]
            outs.append(gate[:, sl] * on)
    return jnp.concatenate(outs, axis=1).astype(BF16)


def _mix_ffn_kernel(*refs, a_kind, b_kind, final_norm):
    refs = list(refs)
    x_ref, a_ref, b_ref = refs[:3]
    del refs[:3]
    a_aux = refs.pop(0) if a_kind in ("gate_rms", "gn_gate") else None
    b_aux = refs.pop(0) if b_kind in ("gate_rms", "gn_gate") else None
    ag_ref, ab_ref, bg_ref, bb_ref, wa_ref, wb_ref, g_ref, wg_ref, wu_ref, wd_ref, fg_ref, o_ref, act_ref = refs
    a = _finish_mixer(a_ref, a_aux, ag_ref, ab_ref, a_kind)
    b = _finish_mixer(b_ref, b_aux, bg_ref, bb_ref, b_kind)
    x = x_ref[...] + _dot(a, wa_ref[...]) + _dot(b, wb_ref[...])
    h = _rms(x, g_ref[...]).astype(BF16)
    for k in range(wg_ref.shape[1] // FFN_COLS):
        sl = slice(k * FFN_COLS, (k + 1) * FFN_COLS)
        gate = _dot(h, wg_ref[:, sl])
        up = _dot(h, wu_ref[:, sl])
        act_ref[:, sl] = (gate * jax.nn.sigmoid(gate) * up).astype(BF16)
    y = x + _dot(act_ref[...], wd_ref[...])
    if final_norm:
        y = _rms(y, fg_ref[...])
    o_ref[...] = y


def _mix_ffn(x, a, a_tail, b, b_tail, w_out, g, w_gate, w_up, w_down, final_g, name):
    m, d = x.shape
    ka, kb = a.shape[1], b.shape[1]
    f = w_gate.shape[1]
    tm = FFN_ROWS
    final_norm = final_g is not None
    fg = (final_g if final_norm else jnp.ones((d,), F32)).astype(F32).reshape(1, d)
    resident = dict(pipeline_mode=pl.Buffered(1))
    row_block = lambda k: pl.BlockSpec((tm, k), lambda i: (i, 0))
    vec_block = lambda k: pl.BlockSpec((1, k), lambda i: (0, 0))
    args = [x, a, b]
    in_specs = [row_block(d), row_block(ka), row_block(kb)]
    kinds, vecs = [], []
    for tail, k in ((a_tail, ka), (b_tail, kb)):
        kind, aux, gain, bias = tail if tail is not None else (None, None, None, None)
        kinds.append(kind)
        if aux is not None:
            arr, blk = aux
            args.append(arr)
            in_specs.append(pl.BlockSpec((tm, k), lambda i, blk=blk: (i, blk)))
        gain = jnp.ones((k,), F32) if gain is None else gain
        bias = jnp.zeros((k,), F32) if bias is None else bias
        vecs += [gain.astype(F32).reshape(1, k), bias.astype(F32).reshape(1, k)]
    args += vecs + [w_out[:ka].astype(BF16), w_out[ka:].astype(BF16), g.astype(F32).reshape(1, d),
                    w_gate.astype(BF16), w_up.astype(BF16), w_down.astype(BF16), fg]
    in_specs += [vec_block(ka), vec_block(ka), vec_block(kb), vec_block(kb),
                 pl.BlockSpec((ka, d), lambda i: (0, 0), **resident),
                 pl.BlockSpec((kb, d), lambda i: (0, 0), **resident),
                 vec_block(d),
                 pl.BlockSpec((d, f), lambda i: (0, 0), **resident),
                 pl.BlockSpec((d, f), lambda i: (0, 0), **resident),
                 pl.BlockSpec((f, d), lambda i: (0, 0), **resident),
                 vec_block(d)]
    return pl.pallas_call(
        functools.partial(_mix_ffn_kernel, a_kind=kinds[0], b_kind=kinds[1], final_norm=final_norm),
        grid=(m // tm,),
        in_specs=in_specs,
        out_specs=pl.BlockSpec((tm, d), lambda i: (i, 0)),
        out_shape=jax.ShapeDtypeStruct((m, d), F32),
        scratch_shapes=[pltpu.VMEM((tm, f), BF16)],
        compiler_params=_cparams("parallel"),
        name=name,
    )(*args)


def _s5_kernel(u_ref, bre_ref, bim_ref, cre_ref, cim_ref, lre_ref, lim_ref, d_ref,
               gw_ref, gb_ref, o_ref, sre_ref, sim_ref, st_ref, rl_ref):
    nb, ts, dim = u_ref.shape
    rows = nb * ts
    half_in = dim // 2

    @pl.when(pl.program_id(0) == 0)
    def _():
        st_ref[...] = jnp.zeros(st_ref.shape, F32)

    for b in range(nb):
        ub = u_ref[b].astype(F32)
        for k in range(dim // LANES):
            rl_ref[k, pl.ds(b, ts, stride=nb), :] = ub[:, k * LANES:(k + 1) * LANES]
    u_tm = jnp.concatenate([rl_ref[k] for k in range(dim // LANES)], axis=1)
    u16 = u_tm.astype(BF16)

    for hf in range(2):
        uh = u16[:, hf * half_in:(hf + 1) * half_in]
        sre_ref[:, hf * S5_HALF:(hf + 1) * S5_HALF] = _dot(uh, bre_ref[hf])
        sim_ref[:, hf * S5_HALF:(hf + 1) * S5_HALF] = _dot(uh, bim_ref[hf])

    for hf in range(2):
        sl = slice(hf * S5_HALF, (hf + 1) * S5_HALF)
        lr = lre_ref[:, sl]
        li = lim_ref[:, sl]

        def step(t, carry, sl=sl, lr=lr, li=li):
            sr, si = carry
            r0 = pl.multiple_of(t * nb, nb)
            nr = lr * sr - li * si + sre_ref[pl.ds(r0, nb), sl]
            ni = lr * si + li * sr + sim_ref[pl.ds(r0, nb), sl]
            sre_ref[pl.ds(r0, nb), sl] = nr
            sim_ref[pl.ds(r0, nb), sl] = ni
            return nr, ni

        sr, si = lax.fori_loop(0, ts, step, (st_ref[0, :, sl], st_ref[1, :, sl]), unroll=True)
        st_ref[0, :, sl] = sr
        st_ref[1, :, sl] = si

    ys = []
    for hf in range(2):
        sl = slice(hf * S5_HALF, (hf + 1) * S5_HALF)
        ys.append(_dot(sre_ref[:, sl].astype(BF16), cre_ref[hf]) + _dot(sim_ref[:, sl].astype(BF16), cim_ref[hf]))
    y = jnp.concatenate(ys, axis=1) + d_ref[...] * u_tm
    y = jax.nn.gelu(y)
    y = y * jax.nn.sigmoid(_dot(y.astype(BF16), gw_ref[...]) + gb_ref[...])
    for k in range(dim // LANES):
        rl_ref[k] = y[:, k * LANES:(k + 1) * LANES]
    for b in range(nb):
        for k in range(dim // LANES):
            o_ref[b, :, k * LANES:(k + 1) * LANES] = rl_ref[k, pl.ds(b, ts, stride=nb), :].astype(o_ref.dtype)


def _s5_mixer(u_wide3, lam_re, lam_im, b_re, b_im, c_re, c_im, log_dt, d_skip, glu_w, glu_b, batch, seq):
    ts = S5_STEPS
    rows = batch * ts
    G, P, C = S5_GROUPS, S5_STATE, S5_GROUP
    hg = G // 2
    lam = lax.complex(lam_re.astype(F32), lam_im.astype(F32))
    dt = jnp.exp(log_dt.astype(F32))[:, None]
    lam_bar = jnp.exp(lam * dt)
    b_bar = ((lam_bar - 1.0) / lam)[..., None] * lax.complex(b_re.astype(F32), b_im.astype(F32))
    eye = jnp.eye(hg, dtype=F32)

    def blockdiag_in(v):
        blk = v.transpose(0, 2, 1).reshape(2, hg, C, P)
        return jnp.einsum('hgcp,gk->hgckp', blk, eye).reshape(2, hg * C, hg * P).astype(BF16)

    def blockdiag_out(v):
        blk = v.transpose(0, 2, 1).reshape(2, hg, P, C)
        return jnp.einsum('hgpc,gk->hgpkc', blk, eye).reshape(2, hg * P, hg * C).astype(BF16)

    bre = blockdiag_in(jnp.real(b_bar))
    bim = blockdiag_in(jnp.imag(b_bar))
    cre = blockdiag_out(c_re.astype(F32))
    cim = blockdiag_out(-c_im.astype(F32))
    lre = jnp.broadcast_to(jnp.real(lam_bar).reshape(1, G * P), (batch, G * P))
    lim = jnp.broadcast_to(jnp.imag(lam_bar).reshape(1, G * P), (batch, G * P))
    ublk = u_wide3.shape[2] // S5_DIM - 1
    const2 = lambda i: (0, 0)
    const3 = lambda i: (0, 0, 0)
    return pl.pallas_call(
        _s5_kernel,
        grid=(seq // ts,),
        in_specs=[
            pl.BlockSpec((batch, ts, S5_DIM), lambda i: (0, i, ublk)),
            pl.BlockSpec((2, hg * C, hg * P), const3),
            pl.BlockSpec((2, hg * C, hg * P), const3),
            pl.BlockSpec((2, hg * P, hg * C), const3),
            pl.BlockSpec((2, hg * P, hg * C), const3),
            pl.BlockSpec((batch, G * P), const2),
            pl.BlockSpec((batch, G * P), const2),
            pl.BlockSpec((1, S5_DIM), const2),
            pl.BlockSpec((S5_DIM, S5_DIM), const2),
            pl.BlockSpec((1, S5_DIM), const2),
        ],
        out_specs=pl.BlockSpec((batch, ts, S5_DIM), lambda i: (0, i, 0)),
        out_shape=jax.ShapeDtypeStruct((batch, seq, S5_DIM), ACT_DTYPE),
        scratch_shapes=[pltpu.VMEM((rows, S5_LANES), F32), pltpu.VMEM((rows, S5_LANES), F32),
                        pltpu.VMEM((2, batch, S5_LANES), F32),
                        pltpu.VMEM((S5_DIM // LANES, rows, LANES), F32)],
        compiler_params=_cparams("arbitrary"),
        name="s5_mixer",
    )(u_wide3, bre, bim, cre, cim, lre, lim, d_skip.astype(F32).reshape(1, S5_DIM),
      glu_w.astype(BF16), glu_b.astype(F32).reshape(1, S5_DIM))


def _ret_kernel(q_ref, k_ref, v_ref, cos_ref, sin_ref, dmat_ref, qdec_ref, kdec_ref, cg_ref, o_ref, state_ref):
    L = RET_CHUNK
    dk, dv = RET_KEY_DIM, RET_VAL_DIM
    half = dk // 2

    @pl.when(pl.program_id(1) == 0)
    def _():
        state_ref[...] = jnp.zeros(state_ref.shape, F32)

    def rot(t, cos, sin):
        t1, t2 = t[:, :half], t[:, half:]
        return jnp.concatenate([t1 * cos - t2 * sin, t1 * sin + t2 * cos], axis=1)

    for ci in range(q_ref.shape[0] // L):
        rs = slice(ci * L, (ci + 1) * L)
        cos = cos_ref[rs, :]
        sin = sin_ref[rs, :]
        for h in range(RET_HEADS):
            q = rot(q_ref[rs, h * dk:(h + 1) * dk].astype(F32), cos, sin)
            k = rot(k_ref[rs, h * dk:(h + 1) * dk].astype(F32), cos, sin) * (dk ** -0.5)
            v = v_ref[rs, h * dv:(h + 1) * dv]
            q16 = q.astype(BF16)
            k_t = k.T
            scores = _dot(q16, k_t.astype(BF16)) * dmat_ref[h]
            prev = state_ref[h]
            o = _dot(scores.astype(BF16), v) + _dot(q16, prev.astype(BF16)) * qdec_ref[:, h:h + 1]
            kv = _dot((k_t * kdec_ref[h:h + 1, :]).astype(BF16), v)
            state_ref[h] = prev * cg_ref[h:h + 1, :] + kv
            o_ref[rs, h * dv:(h + 1) * dv] = o.astype(o_ref.dtype)


def _ret_mixer(u_wide, batch, seq):
    L = RET_CHUNK
    m = batch * seq
    inv_freq = ROPE_BASE ** (-jnp.arange(0, RET_KEY_DIM, 2, dtype=F32) / RET_KEY_DIM)
    ang = jnp.arange(seq, dtype=F32)[:, None] * inv_freq[None, :]
    cos, sin = jnp.cos(ang), jnp.sin(ang)
    gamma = 1.0 - jnp.exp(jnp.linspace(math.log(1.0 / 32), math.log(1.0 / 512), RET_HEADS, dtype=F32))
    log_g = jnp.log(gamma)
    idx = jnp.arange(L, dtype=F32)
    diff = idx[:, None] - idx[None, :]
    dmat = jnp.where(diff >= 0, jnp.exp(log_g[:, None, None] * jnp.maximum(diff, 0.0)), 0.0)
    kdec = jnp.exp(log_g[:, None] * (L - 1.0 - idx)[None, :])
    kdec = jnp.pad(kdec, ((0, SUBLANES - RET_HEADS), (0, 0)))
    qdec = jnp.exp(log_g[None, :] * (idx + 1.0)[:, None])
    qdec = jnp.pad(qdec, ((0, 0), (0, LANES - RET_HEADS)))
    cg = jnp.broadcast_to(jnp.exp(log_g * L)[:, None], (RET_HEADS, RET_VAL_DIM))
    cg = jnp.pad(cg, ((0, SUBLANES - RET_HEADS), (0, 0)))
    T = RET_STEP_CHUNKS * L
    ns = seq // T
    row = lambda b, c: b * ns + c
    return pl.pallas_call(
        _ret_kernel,
        grid=(batch, ns),
        in_specs=[
            pl.BlockSpec((T, RET_QK), lambda b, c: (row(b, c), 0)),
            pl.BlockSpec((T, RET_QK), lambda b, c: (row(b, c), 1)),
            pl.BlockSpec((T, RET_V), lambda b, c: (row(b, c), 2)),
            pl.BlockSpec((T, RET_KEY_DIM // 2), lambda b, c: (c, 0)),
            pl.BlockSpec((T, RET_KEY_DIM // 2), lambda b, c: (c, 0)),
            pl.BlockSpec((RET_HEADS, L, L), lambda b, c: (0, 0, 0)),
            pl.BlockSpec((L, LANES), lambda b, c: (0, 0)),
            pl.BlockSpec((SUBLANES, L), lambda b, c: (0, 0)),
            pl.BlockSpec((SUBLANES, RET_VAL_DIM), lambda b, c: (0, 0)),
        ],
        out_specs=pl.BlockSpec((T, RET_V), lambda b, c: (row(b, c), 0)),
        out_shape=jax.ShapeDtypeStruct((m, RET_V), ACT_DTYPE),
        scratch_shapes=[pltpu.VMEM((RET_HEADS, RET_KEY_DIM, RET_VAL_DIM), F32)],
        compiler_params=_cparams("parallel", "arbitrary"),
        name="retention_mixer",
    )(u_wide, u_wide, u_wide, cos, sin, dmat, qdec, kdec, cg)


def kernel(x, ev_norm, ev_w_in, ev_conv_w, ev_conv_b, ev_dt_bias, ev_a_log, ev_d, ev_ssd_norm, ev_cf_conv_w, ev_cf_conv_b, ev_cf_ln_g, ev_cf_ln_b, ev_w_out, od_norm, od_w_in, od_lam_re, od_lam_im, od_b_re, od_b_im, od_c_re, od_c_im, od_log_dt, od_s5_d, od_glu_w, od_glu_b, od_gn_g, od_gn_b, od_w_out, ffn_norm, ffn_w_gate, ffn_w_up, ffn_w_down, final_norm):
    batch, seq, d = x.shape
    m = batch * seq
    xf = x.reshape(m, d)

    w = ev_w_in[0]
    o1 = SSD_INNER
    o2 = o1 + SSD_XBC
    o3 = o2 + SSD_HEADS
    w_main = jnp.concatenate([w[:, o1:o2], w[:, o3:], w[:, :o1]], axis=1).astype(BF16)
    w_dt = jnp.pad(w[:, o2:o3], ((0, 0), (0, LANES - SSD_HEADS))).astype(BF16)
    u_wide, dt_raw = _norm_proj(xf, ev_norm[0], w_main, w_dt, name="even_in_proj")
    ya = _ssd_mixer(u_wide, dt_raw, ev_conv_w[0], ev_conv_b[0], ev_dt_bias[0], ev_a_log[0], ev_d[0], batch, seq)
    yb = _conf_mixer(u_wide, ev_cf_conv_w[0], ev_cf_conv_b[0], batch, seq)
    z_block = (SSD_XBC + 2 * CONF_DIM) // SSD_INNER
    xf = _mix_ffn(xf, ya, ("gate_rms", (u_wide, z_block), ev_ssd_norm[0], None),
                  yb, ("ln_swish", None, ev_cf_ln_g[0], ev_cf_ln_b[0]),
                  ev_w_out[0], ffn_norm[0], ffn_w_gate[0], ffn_w_up[0], ffn_w_down[0], None, "even_out_ffn")

    w = od_w_in[0]
    w_main = jnp.concatenate([w[:, S5_DIM:], w[:, :S5_DIM]], axis=1).astype(BF16)
    (u_wide,) = _norm_proj(xf, od_norm[0], w_main, None, name="odd_in_proj")
    yc = _s5_mixer(u_wide.reshape(batch, seq, -1), od_lam_re[0], od_lam_im[0], od_b_re[0], od_b_im[0],
                   od_c_re[0], od_c_im[0], od_log_dt[0], od_s5_d[0], od_glu_w[0], od_glu_b[0], batch, seq)
    yd = _ret_mixer(u_wide, batch, seq)
    gate_block = (2 * RET_QK + RET_V) // RET_V
    xf = _mix_ffn(xf, yc.reshape(m, S5_DIM), None,
                  yd, ("gn_gate", (u_wide, gate_block), od_gn_g[0], od_gn_b[0]),
                  od_w_out[0], ffn_norm[1], ffn_w_gate[1], ffn_w_up[1], ffn_w_down[1], final_norm, "odd_out_ffn")
    return xf.reshape(batch, seq, d)
```

```python
import functools
import math

import numpy as np
import jax
import jax.numpy as jnp
from jax import lax
from jax.experimental import pallas as pl
from jax.experimental.pallas import tpu as pltpu

F32 = jnp.float32
BF16 = jnp.bfloat16
ACT_DTYPE = BF16

D_MODEL = 1024
EPS = 1e-5

SSD_HEADS = 16
SSD_HEAD_DIM = 64
SSD_INNER = SSD_HEADS * SSD_HEAD_DIM
SSD_GROUPS = 4
SSD_STATE = 128
SSD_CONV = 4
SSD_CHUNK = 128
SSD_XBC = SSD_INNER + 2 * SSD_GROUPS * SSD_STATE

CONF_DIM = 1024
CONF_KERNEL = 31

S5_DIM = 512
S5_GROUP = 16
S5_GROUPS = S5_DIM // S5_GROUP
S5_STATE = 64
S5_LANES = S5_GROUPS * S5_STATE
S5_HALF = S5_LANES // 2

RET_HEADS = 4
RET_KEY_DIM = 256
RET_VAL_DIM = 256
RET_CHUNK = 128
RET_QK = RET_HEADS * RET_KEY_DIM
RET_V = RET_HEADS * RET_VAL_DIM
ROPE_BASE = 10000.0

LANES = 128
SUBLANES = 8
VMEM_LIMIT_BYTES = 56 * 1024 * 1024

PROJ_ROWS = 512
PROJ_COLS = 512
FFN_ROWS = 512
FFN_COLS = 256
CONF_ROWS = 256
CONF_HALO = 32
S5_STEPS = 128
RET_STEP_CHUNKS = 8
SSD_STEP_CHUNKS = 8


def _cparams(*sem):
    return pltpu.CompilerParams(dimension_semantics=sem, vmem_limit_bytes=VMEM_LIMIT_BYTES)


def _dot(a, b):
    return jnp.dot(a, b, preferred_element_type=F32)


def _split3(x):
    x1 = x.astype(BF16)
    r1 = x - x1.astype(F32)
    x2 = r1.astype(BF16)
    x3 = (r1 - x2.astype(F32)).astype(BF16)
    return x1, x2, x3


def _rms(x, g):
    return x * lax.rsqrt(jnp.mean(x * x, axis=-1, keepdims=True) + EPS) * g


def _norm_proj_kernel(*refs, has_aux):
    if has_aux:
        x_ref, g_ref, w_ref, waux_ref, o_ref, oaux_ref = refs
    else:
        x_ref, g_ref, w_ref, o_ref = refs
    h = _rms(x_ref[...], g_ref[...]).astype(BF16)
    for k in range(w_ref.shape[1] // PROJ_COLS):
        sl = slice(k * PROJ_COLS, (k + 1) * PROJ_COLS)
        o_ref[:, sl] = _dot(h, w_ref[:, sl]).astype(o_ref.dtype)
    if has_aux:
        oaux_ref[...] = _dot(h, waux_ref[...])


def _norm_proj(x, g, w, w_aux, *, name):
    m, d = x.shape
    n = w.shape[1]
    tm = PROJ_ROWS
    has_aux = w_aux is not None
    resident = dict(pipeline_mode=pl.Buffered(1))
    in_specs = [pl.BlockSpec((tm, d), lambda i: (i, 0)),
                pl.BlockSpec((1, d), lambda i: (0, 0)),
                pl.BlockSpec((d, n), lambda i: (0, 0), **resident)]
    out_specs = [pl.BlockSpec((tm, n), lambda i: (i, 0))]
    out_shape = [jax.ShapeDtypeStruct((m, n), ACT_DTYPE)]
    args = [x, g.reshape(1, d), w]
    if has_aux:
        in_specs.append(pl.BlockSpec((d, LANES), lambda i: (0, 0), **resident))
        out_specs.append(pl.BlockSpec((tm, LANES), lambda i: (i, 0)))
        out_shape.append(jax.ShapeDtypeStruct((m, LANES), F32))
        args.append(w_aux)
    return pl.pallas_call(
        functools.partial(_norm_proj_kernel, has_aux=has_aux),
        grid=(m // tm,),
        in_specs=in_specs, out_specs=out_specs, out_shape=out_shape,
        compiler_params=_cparams("parallel"),
        name=name,
    )(*args)


def _ssd_kernel(xbc_ref, dt_ref, cw_ref, cb_ref, dtb_ref, alog_ref, dskip_ref, e_ref, o_ref, tail_ref, state_ref):
    L = SSD_CHUNK
    N = SSD_STATE
    gw = SSD_INNER // SSD_GROUPS

    @pl.when(pl.program_id(1) == 0)
    def _():
        tail_ref[...] = jnp.zeros(tail_ref.shape, tail_ref.dtype)
        state_ref[...] = jnp.zeros(state_ref.shape, F32)

    lane = lax.broadcasted_iota(jnp.int32, (L, LANES), 1)
    low = lane < SSD_HEAD_DIM
    row = lax.broadcasted_iota(jnp.int32, (L, L), 0)
    col = lax.broadcasted_iota(jnp.int32, (L, L), 1)
    tril = row >= col
    tri = tril.astype(BF16)

    def per_head_to_lanes(v):
        v1 = v.astype(BF16)
        v2 = (v - v1.astype(F32)).astype(BF16)
        return _dot(v1, e_ref[...]) + _dot(v2, e_ref[...])

    for ci in range(xbc_ref.shape[0] // L):
        rs = slice(ci * L, (ci + 1) * L)
        x_in = xbc_ref[rs, :].astype(F32)
        ext = jnp.concatenate([tail_ref[...], x_in], axis=0)
        tail_ref[...] = x_in[L - SUBLANES:, :]
        acc = cb_ref[...] + cw_ref[SSD_CONV - 1:SSD_CONV, :] * x_in
        for d in range(1, SSD_CONV):
            acc = acc + cw_ref[SSD_CONV - 1 - d:SSD_CONV - d, :] * ext[SUBLANES - d:SUBLANES - d + L]
        xbc = acc * jax.nn.sigmoid(acc)
        xs = xbc[:, :SSD_INNER]
        bm = xbc[:, SSD_INNER:SSD_INNER + SSD_GROUPS * N]
        cm = xbc[:, SSD_INNER + SSD_GROUPS * N:]

        dt = jax.nn.softplus(dt_ref[rs, :] + dtb_ref[...])
        a = jnp.where(lane < SSD_HEADS, dt * (-jnp.exp(alog_ref[...])), 0.0)
        a1, a2, a3 = _split3(a)
        a_cs = _dot(tri, a1) + _dot(tri, a2) + _dot(tri, a3)
        a_cs_t = a_cs.T
        ea = jnp.exp(a_cs)
        ds = jnp.exp(a_cs[L - 1:L, :] - a_cs)
        dt_x = per_head_to_lanes(dt)
        dtds_x = per_head_to_lanes(dt * ds)
        ea_x = per_head_to_lanes(ea)

        y_tiles = []
        for g in range(SSD_GROUPS):
            bg = bm[:, g * N:(g + 1) * N]
            cg = cm[:, g * N:(g + 1) * N].astype(BF16)
            bg_t = bg.T.astype(BF16)
            cb = _dot(cg, bg_t)
            prev = state_ref[g]
            y_off = _dot(cg, prev.astype(BF16))
            xd_tiles = []
            for jj in range(2):
                j = 2 * g + jj
                ps = slice(j * LANES, (j + 1) * LANES)
                xs_p = xs[:, ps]
                xdt_p = xs_p * dt_x[:, ps]
                scores = []
                for h in (2 * j, 2 * j + 1):
                    diff = a_cs[:, h:h + 1] - a_cs_t[h:h + 1, :]
                    lmat = jnp.exp(jnp.where(tril, diff, -jnp.inf))
                    scores.append((cb * lmat).astype(BF16))
                lhs = jnp.concatenate(scores, axis=1)
                rhs = jnp.concatenate([jnp.where(low, xdt_p, 0.0), jnp.where(low, 0.0, xdt_p)],
                                      axis=0).astype(BF16)
                y_diag = _dot(lhs, rhs)
                y_tiles.append(y_diag + y_off[:, jj * LANES:(jj + 1) * LANES] * ea_x[:, ps]
                               + dskip_ref[:, ps] * xs_p)
                xd_tiles.append((xs_p * dtds_x[:, ps]).astype(BF16))
            xd = jnp.concatenate(xd_tiles, axis=1)
            decay = ea_x[L - 1:L, g * gw:(g + 1) * gw]
            state_ref[g] = prev * decay + _dot(bg_t, xd)

        o_ref[rs, :] = jnp.concatenate(y_tiles, axis=1).astype(o_ref.dtype)


def _ssd_mixer(u_wide, dt_raw, conv_w, conv_b, dt_bias, a_log, d_skip, batch, seq):
    L = SSD_CHUNK
    T = SSD_STEP_CHUNKS * L
    ns = seq // T
    m = batch * seq

    def pad_heads(v):
        return jnp.pad(v.astype(F32), (0, LANES - SSD_HEADS)).reshape(1, LANES)

    dskip = jnp.repeat(d_skip.astype(F32), SSD_HEAD_DIM).reshape(1, SSD_INNER)
    e_np = np.zeros((LANES, SSD_INNER), np.float32)
    for hh in range(SSD_HEADS):
        e_np[hh, hh * SSD_HEAD_DIM:(hh + 1) * SSD_HEAD_DIM] = 1.0
    head_to_lanes = jnp.asarray(e_np, dtype=BF16)
    return pl.pallas_call(
        _ssd_kernel,
        grid=(batch, ns),
        in_specs=[
            pl.BlockSpec((T, SSD_XBC), lambda b, c: (b * ns + c, 0)),
            pl.BlockSpec((T, LANES), lambda b, c: (b * ns + c, 0)),
            pl.BlockSpec((SSD_CONV, SSD_XBC), lambda b, c: (0, 0)),
            pl.BlockSpec((1, SSD_XBC), lambda b, c: (0, 0)),
            pl.BlockSpec((1, LANES), lambda b, c: (0, 0)),
            pl.BlockSpec((1, LANES), lambda b, c: (0, 0)),
            pl.BlockSpec((1, SSD_INNER), lambda b, c: (0, 0)),
            pl.BlockSpec((LANES, SSD_INNER), lambda b, c: (0, 0)),
        ],
        out_specs=pl.BlockSpec((T, SSD_INNER), lambda b, c: (b * ns + c, 0)),
        out_shape=jax.ShapeDtypeStruct((m, SSD_INNER), ACT_DTYPE),
        scratch_shapes=[pltpu.VMEM((SUBLANES, SSD_XBC), F32),
                        pltpu.VMEM((SSD_GROUPS, SSD_STATE, SSD_INNER // SSD_GROUPS), F32)],
        compiler_params=_cparams("parallel", "arbitrary"),
        name="ssd_mixer",
    )(u_wide, dt_raw, conv_w.astype(F32), conv_b.astype(F32).reshape(1, SSD_XBC),
      pad_heads(dt_bias), pad_heads(a_log), dskip, head_to_lanes)


def _conf_kernel(ag_ref, w_ref, b_ref, o_ref, sh_ref, y_ref):
    T = CONF_ROWS
    H = CONF_HALO
    strips = CONF_DIM // LANES

    @pl.when(pl.program_id(1) == 0)
    def _():
        sh_ref[...] = jnp.zeros(sh_ref.shape, F32)

    a = ag_ref[:, :CONF_DIM].astype(F32)
    gate = ag_ref[:, CONF_DIM:].astype(F32)
    h = a * jax.nn.sigmoid(gate)
    for s in range(SUBLANES):
        for c in range(strips):
            sh_ref[s, c, 0:H, :] = sh_ref[s, c, T:T + H, :]
            sh_ref[s, c, pl.ds(H - s, T), :] = h[:, c * LANES:(c + 1) * LANES]

    base = H - (CONF_KERNEL - 1)
    nt = 16
    groups = T // (nt * SUBLANES)

    def strip_group(gi, carry):
        c = gi // groups
        r0 = pl.multiple_of((gi % groups) * (nt * SUBLANES), nt * SUBLANES)
        tiles = [jnp.broadcast_to(b_ref[c], (SUBLANES, LANES))] * nt
        for j in range(CONF_KERNEL):
            q, s = divmod(base + j, SUBLANES)
            wj = jnp.broadcast_to(w_ref[c, j:j + 1, :], (SUBLANES, LANES))
            tiles = [t + wj * sh_ref[s, c, pl.ds(r0 + (q + i) * SUBLANES, SUBLANES), :]
                     for i, t in enumerate(tiles)]
        y_ref[c, pl.ds(r0, nt * SUBLANES), :] = jnp.concatenate(tiles, axis=0)
        return carry

    lax.fori_loop(0, strips * groups, strip_group, 0)

    for c in range(strips):
        o_ref[:, c * LANES:(c + 1) * LANES] = y_ref[c].astype(o_ref.dtype)


def _conf_mixer(u_wide, conv_w, conv_b, batch, seq):
    T = CONF_ROWS
    nt = seq // T
    m = batch * seq
    strips = CONF_DIM // LANES
    taps_pad = CONF_KERNEL + 1
    w_strips = jnp.pad(conv_w.astype(F32), ((0, 1), (0, 0))).reshape(taps_pad, strips, LANES).transpose(1, 0, 2)
    b_strips = conv_b.astype(F32).reshape(strips, 1, LANES)
    return pl.pallas_call(
        _conf_kernel,
        grid=(batch, nt),
        in_specs=[
            pl.BlockSpec((T, 2 * CONF_DIM), lambda b, c: (b * nt + c, 1)),
            pl.BlockSpec((strips, taps_pad, LANES), lambda b, c: (0, 0, 0)),
            pl.BlockSpec((strips, 1, LANES), lambda b, c: (0, 0, 0)),
        ],
        out_specs=pl.BlockSpec((T, CONF_DIM), lambda b, c: (b * nt + c, 0)),
        out_shape=jax.ShapeDtypeStruct((m, CONF_DIM), ACT_DTYPE),
        scratch_shapes=[pltpu.VMEM((SUBLANES, CONF_DIM // LANES, T + CONF_HALO, LANES), F32),
                        pltpu.VMEM((CONF_DIM // LANES, T, LANES), F32)],
        compiler_params=_cparams("parallel", "arbitrary"),
        name="conformer_mixer",
    )(u_wide, w_strips, b_strips)


def _finish_mixer(v_ref, aux_ref, gain_ref, bias_ref, kind):
    if kind is None:
        return v_ref[...]
    v = v_ref[...].astype(F32)
    width = 2 * LANES
    if kind == "ln_swish":
        mu = jnp.mean(v, axis=-1, keepdims=True)
        var = jnp.mean(jnp.square(v - mu), axis=-1, keepdims=True)
        hn = (v - mu) * lax.rsqrt(var + EPS) * gain_ref[...] + bias_ref[...]
        return (hn * jax.nn.sigmoid(hn)).astype(BF16)
    aux = aux_ref[...].astype(F32)
    gate = aux * jax.nn.sigmoid(aux)
    outs = []
    for gi in range(v.shape[1] // width):
        sl = slice(gi * width, (gi + 1) * width)
        if kind == "gate_rms":
            vg = v[:, sl] * gate[:, sl]
            outs.append(vg * lax.rsqrt(jnp.mean(vg * vg, axis=-1, keepdims=True) + EPS) * gain_ref[:, sl])
        else:
            vg = v[:, sl]
            mu = jnp.mean(vg, axis=-1, keepdims=True)
            var = jnp.mean(jnp.square(vg - mu), axis=-1, keepdims=True)
            on = (vg - mu) * lax.rsqrt(var + EPS) * gain_ref[:, sl] + bias_ref[:, sl]
            outs.append(gate[:, sl] * on)
    return jnp.concatenate(outs, axis=1).astype(BF16)


def _mix_ffn_kernel(*refs, a_kind, b_kind, final_norm):
    refs = list(refs)
    x_ref, a_ref, b_ref = refs[:3]
    del refs[:3]
    a_aux = refs.pop(0) if a_kind in ("gate_rms", "gn_gate") else None
    b_aux = refs.pop(0) if b_kind in ("gate_rms", "gn_gate") else None
    ag_ref, ab_ref, bg_ref, bb_ref, wa_ref, wb_ref, g_ref, wg_ref, wu_ref, wd_ref, fg_ref, o_ref, act_ref = refs
    a = _finish_mixer(a_ref, a_aux, ag_ref, ab_ref, a_kind)
    b = _finish_mixer(b_ref, b_aux, bg_ref, bb_ref, b_kind)
    x = x_ref[...] + _dot(a, wa_ref[...]) + _dot(b, wb_ref[...])
    h = _rms(x, g_ref[...]).astype(BF16)
    for k in range(wg_ref.shape[1] // FFN_COLS):
        sl = slice(k * FFN_COLS, (k + 1) * FFN_COLS)
        gate = _dot(h, wg_ref[:, sl])
        up = _dot(h, wu_ref[:, sl])
        act_ref[:, sl] = (gate * jax.nn.sigmoid(gate) * up).astype(BF16)
    y = x + _dot(act_ref[...], wd_ref[...])
    if final_norm:
        y = _rms(y, fg_ref[...])
    o_ref[...] = y


def _mix_ffn(x, a, a_tail, b, b_tail, w_out, g, w_gate, w_up, w_down, final_g, name):
    m, d = x.shape
    ka, kb = a.shape[1], b.shape[1]
    f = w_gate.shape[1]
    tm = FFN_ROWS
    final_norm = final_g is not None
    fg = (final_g if final_norm else jnp.ones((d,), F32)).astype(F32).reshape(1, d)
    resident = dict(pipeline_mode=pl.Buffered(1))
    row_block = lambda k: pl.BlockSpec((tm, k), lambda i: (i, 0))
    vec_block = lambda k: pl.BlockSpec((1, k), lambda i: (0, 0))
    args = [x, a, b]
    in_specs = [row_block(d), row_block(ka), row_block(kb)]
    kinds, vecs = [], []
    for tail, k in ((a_tail, ka), (b_tail, kb)):
        kind, aux, gain, bias = tail if tail is not None else (None, None, None, None)
        kinds.append(kind)
        if aux is not None:
            arr, blk = aux
            args.append(arr)
            in_specs.append(pl.BlockSpec((tm, k), lambda i, blk=blk: (i, blk)))
        gain = jnp.ones((k,), F32) if gain is None else gain
        bias = jnp.zeros((k,), F32) if bias is None else bias
        vecs += [gain.astype(F32).reshape(1, k), bias.astype(F32).reshape(1, k)]
    args += vecs + [w_out[:ka].astype(BF16), w_out[ka:].astype(BF16), g.astype(F32).reshape(1, d),
                    w_gate.astype(BF16), w_up.astype(BF16), w_down.astype(BF16), fg]
    in_specs += [vec_block(ka), vec_block(ka), vec_block(kb), vec_block(kb),
                 pl.BlockSpec((ka, d), lambda i: (0, 0), **resident),
                 pl.BlockSpec((kb, d), lambda i: (0, 0), **resident),
                 vec_block(d),
                 pl.BlockSpec((d, f), lambda i: (0, 0), **resident),
                 pl.BlockSpec((d, f), lambda i: (0, 0), **resident),
                 pl.BlockSpec((f, d), lambda i: (0, 0), **resident),
                 vec_block(d)]
    return pl.pallas_call(
        functools.partial(_mix_ffn_kernel, a_kind=kinds[0], b_kind=kinds[1], final_norm=final_norm),
        grid=(m // tm,),
        in_specs=in_specs,
        out_specs=pl.BlockSpec((tm, d), lambda i: (i, 0)),
        out_shape=jax.ShapeDtypeStruct((m, d), F32),
        scratch_shapes=[pltpu.VMEM((tm, f), BF16)],
        compiler_params=_cparams("parallel"),
        name=name,
    )(*args)


def _s5_kernel(u_ref, bre_ref, bim_ref, cre_ref, cim_ref, lre_ref, lim_ref, d_ref,
               gw_ref, gb_ref, o_ref, sre_ref, sim_ref, st_ref, rl_ref):
    nb, ts, dim = u_ref.shape
    rows = nb * ts
    half_in = dim // 2

    @pl.when(pl.program_id(0) == 0)
    def _():
        st_ref[...] = jnp.zeros(st_ref.shape, F32)

    for b in range(nb):
        ub = u_ref[b].astype(F32)
        for k in range(dim // LANES):
            rl_ref[k, pl.ds(b, ts, stride=nb), :] = ub[:, k * LANES:(k + 1) * LANES]
    u_tm = jnp.concatenate([rl_ref[k] for k in range(dim // LANES)], axis=1)
    u16 = u_tm.astype(BF16)

    for hf in range(2):
        uh = u16[:, hf * half_in:(hf + 1) * half_in]
        sre_ref[:, hf * S5_HALF:(hf + 1) * S5_HALF] = _dot(uh, bre_ref[hf])
        sim_ref[:, hf * S5_HALF:(hf + 1) * S5_HALF] = _dot(uh, bim_ref[hf])

    for hf in range(2):
        sl = slice(hf * S5_HALF, (hf + 1) * S5_HALF)
        lr = lre_ref[:, sl]
        li = lim_ref[:, sl]

        def step(t, carry, sl=sl, lr=lr, li=li):
            sr, si = carry
            r0 = pl.multiple_of(t * nb, nb)
            nr = lr * sr - li * si + sre_ref[pl.ds(r0, nb), sl]
            ni = lr * si + li * sr + sim_ref[pl.ds(r0, nb), sl]
            sre_ref[pl.ds(r0, nb), sl] = nr
            sim_ref[pl.ds(r0, nb), sl] = ni
            return nr, ni

        sr, si = lax.fori_loop(0, ts, step, (st_ref[0, :, sl], st_ref[1, :, sl]), unroll=True)
        st_ref[0, :, sl] = sr
        st_ref[1, :, sl] = si

    ys = []
    for hf in range(2):
        sl = slice(hf * S5_HALF, (hf + 1) * S5_HALF)
        ys.append(_dot(sre_ref[:, sl].astype(BF16), cre_ref[hf]) + _dot(sim_ref[:, sl].astype(BF16), cim_ref[hf]))
    y = jnp.concatenate(ys, axis=1) + d_ref[...] * u_tm
    y = jax.nn.gelu(y)
    y = y * jax.nn.sigmoid(_dot(y.astype(BF16), gw_ref[...]) + gb_ref[...])
    for k in range(dim // LANES):
        rl_ref[k] = y[:, k * LANES:(k + 1) * LANES]
    for b in range(nb):
        for k in range(dim // LANES):
            o_ref[b, :, k * LANES:(k + 1) * LANES] = rl_ref[k, pl.ds(b, ts, stride=nb), :].astype(o_ref.dtype)


def _s5_mixer(u_wide3, lam_re, lam_im, b_re, b_im, c_re, c_im, log_dt, d_skip, glu_w, glu_b, batch, seq):
    ts = S5_STEPS
    rows = batch * ts
    G, P, C = S5_GROUPS, S5_STATE, S5_GROUP
    hg = G // 2
    lam = lax.complex(lam_re.astype(F32), lam_im.astype(F32))
    dt = jnp.exp(log_dt.astype(F32))[:, None]
    lam_bar = jnp.exp(lam * dt)
    b_bar = ((lam_bar - 1.0) / lam)[..., None] * lax.complex(b_re.astype(F32), b_im.astype(F32))
    eye = jnp.eye(hg, dtype=F32)

    def blockdiag_in(v):
        blk = v.transpose(0, 2, 1).reshape(2, hg, C, P)
        return jnp.einsum('hgcp,gk->hgckp', blk, eye).reshape(2, hg * C, hg * P).astype(BF16)

    def blockdiag_out(v):
        blk = v.transpose(0, 2, 1).reshape(2, hg, P, C)
        return jnp.einsum('hgpc,gk->hgpkc', blk, eye).reshape(2, hg * P, hg * C).astype(BF16)

    bre = blockdiag_in(jnp.real(b_bar))
    bim = blockdiag_in(jnp.imag(b_bar))
    cre = blockdiag_out(c_re.astype(F32))
    cim = blockdiag_out(-c_im.astype(F32))
    lre = jnp.broadcast_to(jnp.real(lam_bar).reshape(1, G * P), (batch, G * P))
    lim = jnp.broadcast_to(jnp.imag(lam_bar).reshape(1, G * P), (batch, G * P))
    ublk = u_wide3.shape[2] // S5_DIM - 1
    const2 = lambda i: (0, 0)
    const3 = lambda i: (0, 0, 0)
    return pl.pallas_call(
        _s5_kernel,
        grid=(seq // ts,),
        in_specs=[
            pl.BlockSpec((batch, ts, S5_DIM), lambda i: (0, i, ublk)),
            pl.BlockSpec((2, hg * C, hg * P), const3),
            pl.BlockSpec((2, hg * C, hg * P), const3),
            pl.BlockSpec((2, hg * P, hg * C), const3),
            pl.BlockSpec((2, hg * P, hg * C), const3),
            pl.BlockSpec((batch, G * P), const2),
            pl.BlockSpec((batch, G * P), const2),
            pl.BlockSpec((1, S5_DIM), const2),
            pl.BlockSpec((S5_DIM, S5_DIM), const2),
            pl.BlockSpec((1, S5_DIM), const2),
        ],
        out_specs=pl.BlockSpec((batch, ts, S5_DIM), lambda i: (0, i, 0)),
        out_shape=jax.ShapeDtypeStruct((batch, seq, S5_DIM), ACT_DTYPE),
        scratch_shapes=[pltpu.VMEM((rows, S5_LANES), F32), pltpu.VMEM((rows, S5_LANES), F32),
                        pltpu.VMEM((2, batch, S5_LANES), F32),
                        pltpu.VMEM((S5_DIM // LANES, rows, LANES), F32)],
        compiler_params=_cparams("arbitrary"),
        name="s5_mixer",
    )(u_wide3, bre, bim, cre, cim, lre, lim, d_skip.astype(F32).reshape(1, S5_DIM),
      glu_w.astype(BF16), glu_b.astype(F32).reshape(1, S5_DIM))


def _ret_kernel(q_ref, k_ref, v_ref, cos_ref, sin_ref, dmat_ref, qdec_ref, kdec_ref, cg_ref, o_ref, state_ref):
    L = RET_CHUNK
    dk, dv = RET_KEY_DIM, RET_VAL_DIM
    half = dk // 2

    @pl.when(pl.program_id(1) == 0)
    def _():
        state_ref[...] = jnp.zeros(state_ref.shape, F32)

    def rot(t, cos, sin):
        t1, t2 = t[:, :half], t[:, half:]
        return jnp.concatenate([t1 * cos - t2 * sin, t1 * sin + t2 * cos], axis=1)

    for ci in range(q_ref.shape[0] // L):
        rs = slice(ci * L, (ci + 1) * L)
        cos = cos_ref[rs, :]
        sin = sin_ref[rs, :]
        for h in range(RET_HEADS):
            q = rot(q_ref[rs, h * dk:(h + 1) * dk].astype(F32), cos, sin)
            k = rot(k_ref[rs, h * dk:(h + 1) * dk].astype(F32), cos, sin) * (dk ** -0.5)
            v = v_ref[rs, h * dv:(h + 1) * dv]
            q16 = q.astype(BF16)
            k_t = k.T
            scores = _dot(q16, k_t.astype(BF16)) * dmat_ref[h]
            prev = state_ref[h]
            o = _dot(scores.astype(BF16), v) + _dot(q16, prev.astype(BF16)) * qdec_ref[:, h:h + 1]
            kv = _dot((k_t * kdec_ref[h:h + 1, :]).astype(BF16), v)
            state_ref[h] = prev * cg_ref[h:h + 1, :] + kv
            o_ref[rs, h * dv:(h + 1) * dv] = o.astype(o_ref.dtype)


def _ret_mixer(u_wide, batch, seq):
    L = RET_CHUNK
    m = batch * seq
    inv_freq = ROPE_BASE ** (-jnp.arange(0, RET_KEY_DIM, 2, dtype=F32) / RET_KEY_DIM)
    ang = jnp.arange(seq, dtype=F32)[:, None] * inv_freq[None, :]
    cos, sin = jnp.cos(ang), jnp.sin(ang)
    gamma = 1.0 - jnp.exp(jnp.linspace(math.log(1.0 / 32), math.log(1.0 / 512), RET_HEADS, dtype=F32))
    log_g = jnp.log(gamma)
    idx = jnp.arange(L, dtype=F32)
    diff = idx[:, None] - idx[None, :]
    dmat = jnp.where(diff >= 0, jnp.exp(log_g[:, None, None] * jnp.maximum(diff, 0.0)), 0.0)
    kdec = jnp.exp(log_g[:, None] * (L - 1.0 - idx)[None, :])
    kdec = jnp.pad(kdec, ((0, SUBLANES - RET_HEADS), (0, 0)))
    qdec = jnp.exp(log_g[None, :] * (idx + 1.0)[:, None])
    qdec = jnp.pad(qdec, ((0, 0), (0, LANES - RET_HEADS)))
    cg = jnp.broadcast_to(jnp.exp(log_g * L)[:, None], (RET_HEADS, RET_VAL_DIM))
    cg = jnp.pad(cg, ((0, SUBLANES - RET_HEADS), (0, 0)))
    T = RET_STEP_CHUNKS * L
    ns = seq // T
    row = lambda b, c: b * ns + c
    return pl.pallas_call(
        _ret_kernel,
        grid=(batch, ns),
        in_specs=[
            pl.BlockSpec((T, RET_QK), lambda b, c: (row(b, c), 0)),
            pl.BlockSpec((T, RET_QK), lambda b, c: (row(b, c), 1)),
            pl.BlockSpec((T, RET_V), lambda b, c: (row(b, c), 2)),
            pl.BlockSpec((T, RET_KEY_DIM // 2), lambda b, c: (c, 0)),
            pl.BlockSpec((T, RET_KEY_DIM // 2), lambda b, c: (c, 0)),
            pl.BlockSpec((RET_HEADS, L, L), lambda b, c: (0, 0, 0)),
            pl.BlockSpec((L, LANES), lambda b, c: (0, 0)),
            pl.BlockSpec((SUBLANES, L), lambda b, c: (0, 0)),
            pl.BlockSpec((SUBLANES, RET_VAL_DIM), lambda b, c: (0, 0)),
        ],
        out_specs=pl.BlockSpec((T, RET_V), lambda b, c: (row(b, c), 0)),
        out_shape=jax.ShapeDtypeStruct((m, RET_V), ACT_DTYPE),
        scratch_shapes=[pltpu.VMEM((RET_HEADS, RET_KEY_DIM, RET_VAL_DIM), F32)],
        compiler_params=_cparams("parallel", "arbitrary"),
        name="retention_mixer",
    )(u_wide, u_wide, u_wide, cos, sin, dmat, qdec, kdec, cg)


def kernel(x, ev_norm, ev_w_in, ev_conv_w, ev_conv_b, ev_dt_bias, ev_a_log, ev_d, ev_ssd_norm, ev_cf_conv_w, ev_cf_conv_b, ev_cf_ln_g, ev_cf_ln_b, ev_w_out, od_norm, od_w_in, od_lam_re, od_lam_im, od_b_re, od_b_im, od_c_re, od_c_im, od_log_dt, od_s5_d, od_glu_w, od_glu_b, od_gn_g, od_gn_b, od_w_out, ffn_norm, ffn_w_gate, ffn_w_up, ffn_w_down, final_norm):
    batch, seq, d = x.shape
    m = batch * seq
    xf = x.reshape(m, d)

    w = ev_w_in[0]
    o1 = SSD_INNER
    o2 = o1 + SSD_XBC
    o3 = o2 + SSD_HEADS
    w_main = jnp.concatenate([w[:, o1:o2], w[:, o3:], w[:, :o1]], axis=1).astype(BF16)
    w_dt = jnp.pad(w[:, o2:o3], ((0, 0), (0, LANES - SSD_HEADS))).astype(BF16)
    u_wide, dt_raw = _norm_proj(xf, ev_norm[0], w_main, w_dt, name="even_in_proj")
    ya = _ssd_mixer(u_wide, dt_raw, ev_conv_w[0], ev_conv_b[0], ev_dt_bias[0], ev_a_log[0], ev_d[0], batch, seq)
    yb = _conf_mixer(u_wide, ev_cf_conv_w[0], ev_cf_conv_b[0], batch, seq)
    z_block = (SSD_XBC + 2 * CONF_DIM) // SSD_INNER
    xf = _mix_ffn(xf, ya, ("gate_rms", (u_wide, z_block), ev_ssd_norm[0], None),
                  yb, ("ln_swish", None, ev_cf_ln_g[0], ev_cf_ln_b[0]),
                  ev_w_out[0], ffn_norm[0], ffn_w_gate[0], ffn_w_up[0], ffn_w_down[0], None, "even_out_ffn")

    w = od_w_in[0]
    w_main = jnp.concatenate([w[:, S5_DIM:], w[:, :S5_DIM]], axis=1).astype(BF16)
    (u_wide,) = _norm_proj(xf, od_norm[0], w_main, None, name="odd_in_proj")
    yc = _s5_mixer(u_wide.reshape(batch, seq, -1), od_lam_re[0], od_lam_im[0], od_b_re[0], od_b_im[0],
                   od_c_re[0], od_c_im[0], od_log_dt[0], od_s5_d[0], od_glu_w[0], od_glu_b[0], batch, seq)
    yd = _ret_mixer(u_wide, batch, seq)
    gate_block = (2 * RET_QK + RET_V) // RET_V
    xf = _mix_ffn(xf, yc.reshape(m, S5_DIM), None,
                  yd, ("gn_gate", (u_wide, gate_block), od_gn_g[0], od_gn_b[0]),
                  od_w_out[0], ffn_norm[1], ffn_w_gate[1], ffn_w_up[1], ffn_w_down[1], final_norm, "odd_out_ffn")
    return xf.reshape(batch, seq, d)
```

```python
import functools
import math

import numpy as np
import jax
import jax.numpy as jnp
from jax import lax
from jax.experimental import pallas as pl
from jax.experimental.pallas import tpu as pltpu

F32 = jnp.float32
BF16 = jnp.bfloat16
ACT_DTYPE = BF16

D_MODEL = 1024
EPS = 1e-5

SSD_HEADS = 16
SSD_HEAD_DIM = 64
SSD_INNER = SSD_HEADS * SSD_HEAD_DIM
SSD_GROUPS = 4
SSD_STATE = 128
SSD_CONV = 4
SSD_CHUNK = 128
SSD_XBC = SSD_INNER + 2 * SSD_GROUPS * SSD_STATE

CONF_DIM = 1024
CONF_KERNEL = 31

S5_DIM = 512
S5_GROUP = 16
S5_GROUPS = S5_DIM // S5_GROUP
S5_STATE = 64
S5_LANES = S5_GROUPS * S5_STATE
S5_HALF = S5_LANES // 2

RET_HEADS = 4
RET_KEY_DIM = 256
RET_VAL_DIM = 256
RET_CHUNK = 128
RET_QK = RET_HEADS * RET_KEY_DIM
RET_V = RET_HEADS * RET_VAL_DIM
ROPE_BASE = 10000.0

LANES = 128
SUBLANES = 8
VMEM_LIMIT_BYTES = 56 * 1024 * 1024

PROJ_ROWS = 1024
PROJ_COLS = 512
FFN_ROWS = 512
FFN_COLS = 256
CONF_ROWS = 512
CONF_HALO = 32
S5_STEPS = 128
RET_STEP_CHUNKS = 8
SSD_STEP_CHUNKS = 8


def _cparams(*sem):
    return pltpu.CompilerParams(dimension_semantics=sem, vmem_limit_bytes=VMEM_LIMIT_BYTES)


def _dot(a, b):
    return jnp.dot(a, b, preferred_element_type=F32)


def _split3(x):
    x1 = x.astype(BF16)
    r1 = x - x1.astype(F32)
    x2 = r1.astype(BF16)
    x3 = (r1 - x2.astype(F32)).astype(BF16)
    return x1, x2, x3


def _rms(x, g):
    return x * lax.rsqrt(jnp.mean(x * x, axis=-1, keepdims=True) + EPS) * g


def _norm_proj_kernel(*refs, has_aux):
    if has_aux:
        x_ref, g_ref, w_ref, waux_ref, o_ref, oaux_ref = refs
    else:
        x_ref, g_ref, w_ref, o_ref = refs
    h = _rms(x_ref[...], g_ref[...]).astype(BF16)
    for k in range(w_ref.shape[1] // PROJ_COLS):
        sl = slice(k * PROJ_COLS, (k + 1) * PROJ_COLS)
        o_ref[:, sl] = _dot(h, w_ref[:, sl]).astype(o_ref.dtype)
    if has_aux:
        oaux_ref[...] = _dot(h, waux_ref[...])


def _norm_proj(x, g, w, w_aux, *, name):
    m, d = x.shape
    n = w.shape[1]
    tm = PROJ_ROWS
    has_aux = w_aux is not None
    resident = dict(pipeline_mode=pl.Buffered(1))
    in_specs = [pl.BlockSpec((tm, d), lambda i: (i, 0)),
                pl.BlockSpec((1, d), lambda i: (0, 0)),
                pl.BlockSpec((d, n), lambda i: (0, 0), **resident)]
    out_specs = [pl.BlockSpec((tm, n), lambda i: (i, 0))]
    out_shape = [jax.ShapeDtypeStruct((m, n), ACT_DTYPE)]
    args = [x, g.reshape(1, d), w]
    if has_aux:
        in_specs.append(pl.BlockSpec((d, LANES), lambda i: (0, 0), **resident))
        out_specs.append(pl.BlockSpec((tm, LANES), lambda i: (i, 0)))
        out_shape.append(jax.ShapeDtypeStruct((m, LANES), F32))
        args.append(w_aux)
    return pl.pallas_call(
        functools.partial(_norm_proj_kernel, has_aux=has_aux),
        grid=(m // tm,),
        in_specs=in_specs, out_specs=out_specs, out_shape=out_shape,
        compiler_params=_cparams("parallel"),
        name=name,
    )(*args)


def _ssd_kernel(xbc_ref, dt_ref, cw_ref, cb_ref, dtb_ref, alog_ref, dskip_ref, e_ref, o_ref, tail_ref, state_ref):
    L = SSD_CHUNK
    N = SSD_STATE
    gw = SSD_INNER // SSD_GROUPS

    @pl.when(pl.program_id(1) == 0)
    def _():
        tail_ref[...] = jnp.zeros(tail_ref.shape, tail_ref.dtype)
        state_ref[...] = jnp.zeros(state_ref.shape, F32)

    lane = lax.broadcasted_iota(jnp.int32, (L, LANES), 1)
    low = lane < SSD_HEAD_DIM
    row = lax.broadcasted_iota(jnp.int32, (L, L), 0)
    col = lax.broadcasted_iota(jnp.int32, (L, L), 1)
    tril = row >= col
    tri = tril.astype(BF16)

    def per_head_to_lanes(v):
        v1 = v.astype(BF16)
        v2 = (v - v1.astype(F32)).astype(BF16)
        return _dot(v1, e_ref[...]) + _dot(v2, e_ref[...])

    for ci in range(xbc_ref.shape[0] // L):
        rs = slice(ci * L, (ci + 1) * L)
        x_in = xbc_ref[rs, :].astype(F32)
        ext = jnp.concatenate([tail_ref[...], x_in], axis=0)
        tail_ref[...] = x_in[L - SUBLANES:, :]
        acc = cb_ref[...] + cw_ref[SSD_CONV - 1:SSD_CONV, :] * x_in
        for d in range(1, SSD_CONV):
            acc = acc + cw_ref[SSD_CONV - 1 - d:SSD_CONV - d, :] * ext[SUBLANES - d:SUBLANES - d + L]
        xbc = acc * jax.nn.sigmoid(acc)
        xs = xbc[:, :SSD_INNER]
        bm = xbc[:, SSD_INNER:SSD_INNER + SSD_GROUPS * N]
        cm = xbc[:, SSD_INNER + SSD_GROUPS * N:]

        dt = jax.nn.softplus(dt_ref[rs, :] + dtb_ref[...])
        a = jnp.where(lane < SSD_HEADS, dt * (-jnp.exp(alog_ref[...])), 0.0)
        a1, a2, a3 = _split3(a)
        a_cs = _dot(tri, a1) + _dot(tri, a2) + _dot(tri, a3)
        a_cs_t = a_cs.T
        ea = jnp.exp(a_cs)
        ds = jnp.exp(a_cs[L - 1:L, :] - a_cs)
        dt_x = per_head_to_lanes(dt)
        dtds_x = per_head_to_lanes(dt * ds)
        ea_x = per_head_to_lanes(ea)

        y_tiles = []
        for g in range(SSD_GROUPS):
            bg = bm[:, g * N:(g + 1) * N]
            cg = cm[:, g * N:(g + 1) * N].astype(BF16)
            bg_t = bg.T.astype(BF16)
            cb = _dot(cg, bg_t)
            prev = state_ref[g]
            y_off = _dot(cg, prev.astype(BF16))
            xd_tiles = []
            for jj in range(2):
                j = 2 * g + jj
                ps = slice(j * LANES, (j + 1) * LANES)
                xs_p = xs[:, ps]
                xdt_p = xs_p * dt_x[:, ps]
                scores = []
                for h in (2 * j, 2 * j + 1):
                    diff = a_cs[:, h:h + 1] - a_cs_t[h:h + 1, :]
                    lmat = jnp.exp(jnp.where(tril, diff, -jnp.inf))
                    scores.append((cb * lmat).astype(BF16))
                lhs = jnp.concatenate(scores, axis=1)
                rhs = jnp.concatenate([jnp.where(low, xdt_p, 0.0), jnp.where(low, 0.0, xdt_p)],
                                      axis=0).astype(BF16)
                y_diag = _dot(lhs, rhs)
                y_tiles.append(y_diag + y_off[:, jj * LANES:(jj + 1) * LANES] * ea_x[:, ps]
                               + dskip_ref[:, ps] * xs_p)
                xd_tiles.append((xs_p * dtds_x[:, ps]).astype(BF16))
            xd = jnp.concatenate(xd_tiles, axis=1)
            decay = ea_x[L - 1:L, g * gw:(g + 1) * gw]
            state_ref[g] = prev * decay + _dot(bg_t, xd)

        o_ref[rs, :] = jnp.concatenate(y_tiles, axis=1).astype(o_ref.dtype)


def _ssd_mixer(u_wide, dt_raw, conv_w, conv_b, dt_bias, a_log, d_skip, batch, seq):
    L = SSD_CHUNK
    T = SSD_STEP_CHUNKS * L
    ns = seq // T
    m = batch * seq

    def pad_heads(v):
        return jnp.pad(v.astype(F32), (0, LANES - SSD_HEADS)).reshape(1, LANES)

    dskip = jnp.repeat(d_skip.astype(F32), SSD_HEAD_DIM).reshape(1, SSD_INNER)
    e_np = np.zeros((LANES, SSD_INNER), np.float32)
    for hh in range(SSD_HEADS):
        e_np[hh, hh * SSD_HEAD_DIM:(hh + 1) * SSD_HEAD_DIM] = 1.0
    head_to_lanes = jnp.asarray(e_np, dtype=BF16)
    return pl.pallas_call(
        _ssd_kernel,
        grid=(batch, ns),
        in_specs=[
            pl.BlockSpec((T, SSD_XBC), lambda b, c: (b * ns + c, 0)),
            pl.BlockSpec((T, LANES), lambda b, c: (b * ns + c, 0)),
            pl.BlockSpec((SSD_CONV, SSD_XBC), lambda b, c: (0, 0)),
            pl.BlockSpec((1, SSD_XBC), lambda b, c: (0, 0)),
            pl.BlockSpec((1, LANES), lambda b, c: (0, 0)),
            pl.BlockSpec((1, LANES), lambda b, c: (0, 0)),
            pl.BlockSpec((1, SSD_INNER), lambda b, c: (0, 0)),
            pl.BlockSpec((LANES, SSD_INNER), lambda b, c: (0, 0)),
        ],
        out_specs=pl.BlockSpec((T, SSD_INNER), lambda b, c: (b * ns + c, 0)),
        out_shape=jax.ShapeDtypeStruct((m, SSD_INNER), ACT_DTYPE),
        scratch_shapes=[pltpu.VMEM((SUBLANES, SSD_XBC), F32),
                        pltpu.VMEM((SSD_GROUPS, SSD_STATE, SSD_INNER // SSD_GROUPS), F32)],
        compiler_params=_cparams("parallel", "arbitrary"),
        name="ssd_mixer",
    )(u_wide, dt_raw, conv_w.astype(F32), conv_b.astype(F32).reshape(1, SSD_XBC),
      pad_heads(dt_bias), pad_heads(a_log), dskip, head_to_lanes)


def _conf_kernel(ag_ref, w_ref, b_ref, o_ref, sh_ref, y_ref):
    T = CONF_ROWS
    H = CONF_HALO
    strips = CONF_DIM // LANES

    @pl.when(pl.program_id(1) == 0)
    def _():
        sh_ref[...] = jnp.zeros(sh_ref.shape, F32)

    a = ag_ref[:, :CONF_DIM].astype(F32)
    gate = ag_ref[:, CONF_DIM:].astype(F32)
    h = a * jax.nn.sigmoid(gate)
    for s in range(SUBLANES):
        for c in range(strips):
            sh_ref[s, c, 0:H, :] = sh_ref[s, c, T:T + H, :]
            sh_ref[s, c, pl.ds(H - s, T), :] = h[:, c * LANES:(c + 1) * LANES]

    base = H - (CONF_KERNEL - 1)
    nt = 16
    groups = T // (nt * SUBLANES)

    def strip_group(gi, carry):
        c = gi // groups
        r0 = pl.multiple_of((gi % groups) * (nt * SUBLANES), nt * SUBLANES)
        tiles = [jnp.broadcast_to(b_ref[c], (SUBLANES, LANES))] * nt
        for j in range(CONF_KERNEL):
            q, s = divmod(base + j, SUBLANES)
            wj = jnp.broadcast_to(w_ref[c, j:j + 1, :], (SUBLANES, LANES))
            tiles = [t + wj * sh_ref[s, c, pl.ds(r0 + (q + i) * SUBLANES, SUBLANES), :]
                     for i, t in enumerate(tiles)]
        y_ref[c, pl.ds(r0, nt * SUBLANES), :] = jnp.concatenate(tiles, axis=0)
        return carry

    lax.fori_loop(0, strips * groups, strip_group, 0)

    for c in range(strips):
        o_ref[:, c * LANES:(c + 1) * LANES] = y_ref[c].astype(o_ref.dtype)


def _conf_mixer(u_wide, conv_w, conv_b, batch, seq):
    T = CONF_ROWS
    nt = seq // T
    m = batch * seq
    strips = CONF_DIM // LANES
    taps_pad = CONF_KERNEL + 1
    w_strips = jnp.pad(conv_w.astype(F32), ((0, 1), (0, 0))).reshape(taps_pad, strips, LANES).transpose(1, 0, 2)
    b_strips = conv_b.astype(F32).reshape(strips, 1, LANES)
    return pl.pallas_call(
        _conf_kernel,
        grid=(batch, nt),
        in_specs=[
            pl.BlockSpec((T, 2 * CONF_DIM), lambda b, c: (b * nt + c, 1)),
            pl.BlockSpec((strips, taps_pad, LANES), lambda b, c: (0, 0, 0)),
            pl.BlockSpec((strips, 1, LANES), lambda b, c: (0, 0, 0)),
        ],
        out_specs=pl.BlockSpec((T, CONF_DIM), lambda b, c: (b * nt + c, 0)),
        out_shape=jax.ShapeDtypeStruct((m, CONF_DIM), ACT_DTYPE),
        scratch_shapes=[pltpu.VMEM((SUBLANES, CONF_DIM // LANES, T + CONF_HALO, LANES), F32),
                        pltpu.VMEM((CONF_DIM // LANES, T, LANES), F32)],
        compiler_params=_cparams("parallel", "arbitrary"),
        name="conformer_mixer",
    )(u_wide, w_strips, b_strips)


def _finish_mixer(v_ref, aux_ref, gain_ref, bias_ref, kind):
    if kind is None:
        return v_ref[...]
    v = v_ref[...].astype(F32)
    width = 2 * LANES
    if kind == "ln_swish":
        mu = jnp.mean(v, axis=-1, keepdims=True)
        var = jnp.mean(jnp.square(v - mu), axis=-1, keepdims=True)
        hn = (v - mu) * lax.rsqrt(var + EPS) * gain_ref[...] + bias_ref[...]
        return (hn * jax.nn.sigmoid(hn)).astype(BF16)
    aux = aux_ref[...].astype(F32)
    gate = aux * jax.nn.sigmoid(aux)
    outs = []
    for gi in range(v.shape[1] // width):
        sl = slice(gi * width, (gi + 1) * width)
        if kind == "gate_rms":
            vg = v[:, sl] * gate[:, sl]
            outs.append(vg * lax.rsqrt(jnp.mean(vg * vg, axis=-1, keepdims=True) + EPS) * gain_ref[:, sl])
        else:
            vg = v[:, sl]
            mu = jnp.mean(vg, axis=-1, keepdims=True)
            var = jnp.mean(jnp.square(vg - mu), axis=-1, keepdims=True)
            on = (vg - mu) * lax.rsqrt(var + EPS) * gain_ref[:, sl] + bias_ref[:, sl]
            outs.append(gate[:, sl] * on)
    return jnp.concatenate(outs, axis=1).astype(BF16)


def _mix_ffn_kernel(*refs, a_kind, b_kind, final_norm):
    refs = list(refs)
    x_ref, a_ref, b_ref = refs[:3]
    del refs[:3]
    a_aux = refs.pop(0) if a_kind in ("gate_rms", "gn_gate") else None
    b_aux = refs.pop(0) if b_kind in ("gate_rms", "gn_gate") else None
    ag_ref, ab_ref, bg_ref, bb_ref, wa_ref, wb_ref, g_ref, wg_ref, wu_ref, wd_ref, fg_ref, o_ref, act_ref = refs
    a = _finish_mixer(a_ref, a_aux, ag_ref, ab_ref, a_kind)
    b = _finish_mixer(b_ref, b_aux, bg_ref, bb_ref, b_kind)
    x = x_ref[...] + _dot(a, wa_ref[...]) + _dot(b, wb_ref[...])
    h = _rms(x, g_ref[...]).astype(BF16)
    for k in range(wg_ref.shape[1] // FFN_COLS):
        sl = slice(k * FFN_COLS, (k + 1) * FFN_COLS)
        gate = _dot(h, wg_ref[:, sl])
        up = _dot(h, wu_ref[:, sl])
        act_ref[:, sl] = (gate * jax.nn.sigmoid(gate) * up).astype(BF16)
    y = x + _dot(act_ref[...], wd_ref[...])
    if final_norm:
        y = _rms(y, fg_ref[...])
    o_ref[...] = y


def _mix_ffn(x, a, a_tail, b, b_tail, w_out, g, w_gate, w_up, w_down, final_g, name):
    m, d = x.shape
    ka, kb = a.shape[1], b.shape[1]
    f = w_gate.shape[1]
    tm = FFN_ROWS
    final_norm = final_g is not None
    fg = (final_g if final_norm else jnp.ones((d,), F32)).astype(F32).reshape(1, d)
    resident = dict(pipeline_mode=pl.Buffered(1))
    row_block = lambda k: pl.BlockSpec((tm, k), lambda i: (i, 0))
    vec_block = lambda k: pl.BlockSpec((1, k), lambda i: (0, 0))
    args = [x, a, b]
    in_specs = [row_block(d), row_block(ka), row_block(kb)]
    kinds, vecs = [], []
    for tail, k in ((a_tail, ka), (b_tail, kb)):
        kind, aux, gain, bias = tail if tail is not None else (None, None, None, None)
        kinds.append(kind)
        if aux is not None:
            arr, blk = aux
            args.append(arr)
            in_specs.append(pl.BlockSpec((tm, k), lambda i, blk=blk: (i, blk)))
        gain = jnp.ones((k,), F32) if gain is None else gain
        bias = jnp.zeros((k,), F32) if bias is None else bias
        vecs += [gain.astype(F32).reshape(1, k), bias.astype(F32).reshape(1, k)]
    args += vecs + [w_out[:ka].astype(BF16), w_out[ka:].astype(BF16), g.astype(F32).reshape(1, d),
                    w_gate.astype(BF16), w_up.astype(BF16), w_down.astype(BF16), fg]
    in_specs += [vec_block(ka), vec_block(ka), vec_block(kb), vec_block(kb),
                 pl.BlockSpec((ka, d), lambda i: (0, 0), **resident),
                 pl.BlockSpec((kb, d), lambda i: (0, 0), **resident),
                 vec_block(d),
                 pl.BlockSpec((d, f), lambda i: (0, 0), **resident),
                 pl.BlockSpec((d, f), lambda i: (0, 0), **resident),
                 pl.BlockSpec((f, d), lambda i: (0, 0), **resident),
                 vec_block(d)]
    return pl.pallas_call(
        functools.partial(_mix_ffn_kernel, a_kind=kinds[0], b_kind=kinds[1], final_norm=final_norm),
        grid=(m // tm,),
        in_specs=in_specs,
        out_specs=pl.BlockSpec((tm, d), lambda i: (i, 0)),
        out_shape=jax.ShapeDtypeStruct((m, d), F32),
        scratch_shapes=[pltpu.VMEM((tm, f), BF16)],
        compiler_params=_cparams("parallel"),
        name=name,
    )(*args)


def _s5_kernel(u_ref, bre_ref, bim_ref, cre_ref, cim_ref, lre_ref, lim_ref, d_ref,
               gw_ref, gb_ref, o_ref, sre_ref, sim_ref, st_ref, rl_ref):
    nb, ts, dim = u_ref.shape
    rows = nb * ts
    half_in = dim // 2

    @pl.when(pl.program_id(0) == 0)
    def _():
        st_ref[...] = jnp.zeros(st_ref.shape, F32)

    for b in range(nb):
        ub = u_ref[b].astype(F32)
        for k in range(dim // LANES):
            rl_ref[k, pl.ds(b, ts, stride=nb), :] = ub[:, k * LANES:(k + 1) * LANES]
    u_tm = jnp.concatenate([rl_ref[k] for k in range(dim // LANES)], axis=1)
    u16 = u_tm.astype(BF16)

    for hf in range(2):
        uh = u16[:, hf * half_in:(hf + 1) * half_in]
        sre_ref[:, hf * S5_HALF:(hf + 1) * S5_HALF] = _dot(uh, bre_ref[hf])
        sim_ref[:, hf * S5_HALF:(hf + 1) * S5_HALF] = _dot(uh, bim_ref[hf])

    for hf in range(2):
        sl = slice(hf * S5_HALF, (hf + 1) * S5_HALF)
        lr = lre_ref[:, sl]
        li = lim_ref[:, sl]

        def step(t, carry, sl=sl, lr=lr, li=li):
            sr, si = carry
            r0 = pl.multiple_of(t * nb, nb)
            nr = lr * sr - li * si + sre_ref[pl.ds(r0, nb), sl]
            ni = lr * si + li * sr + sim_ref[pl.ds(r0, nb), sl]
            sre_ref[pl.ds(r0, nb), sl] = nr
            sim_ref[pl.ds(r0, nb), sl] = ni
            return nr, ni

        sr, si = lax.fori_loop(0, ts, step, (st_ref[0, :, sl], st_ref[1, :, sl]), unroll=True)
        st_ref[0, :, sl] = sr
        st_ref[1, :, sl] = si

    ys = []
    for hf in range(2):
        sl = slice(hf * S5_HALF, (hf + 1) * S5_HALF)
        ys.append(_dot(sre_ref[:, sl].astype(BF16), cre_ref[hf]) + _dot(sim_ref[:, sl].astype(BF16), cim_ref[hf]))
    y = jnp.concatenate(ys, axis=1) + d_ref[...] * u_tm
    y = jax.nn.gelu(y)
    y = y * jax.nn.sigmoid(_dot(y.astype(BF16), gw_ref[...]) + gb_ref[...])
    for k in range(dim // LANES):
        rl_ref[k] = y[:, k * LANES:(k + 1) * LANES]
    for b in range(nb):
        for k in range(dim // LANES):
            o_ref[b, :, k * LANES:(k + 1) * LANES] = rl_ref[k, pl.ds(b, ts, stride=nb), :].astype(o_ref.dtype)


def _s5_mixer(u_wide3, lam_re, lam_im, b_re, b_im, c_re, c_im, log_dt, d_skip, glu_w, glu_b, batch, seq):
    ts = S5_STEPS
    rows = batch * ts
    G, P, C = S5_GROUPS, S5_STATE, S5_GROUP
    hg = G // 2
    lam = lax.complex(lam_re.astype(F32), lam_im.astype(F32))
    dt = jnp.exp(log_dt.astype(F32))[:, None]
    lam_bar = jnp.exp(lam * dt)
    b_bar = ((lam_bar - 1.0) / lam)[..., None] * lax.complex(b_re.astype(F32), b_im.astype(F32))
    eye = jnp.eye(hg, dtype=F32)

    def blockdiag_in(v):
        blk = v.transpose(0, 2, 1).reshape(2, hg, C, P)
        return jnp.einsum('hgcp,gk->hgckp', blk, eye).reshape(2, hg * C, hg * P).astype(BF16)

    def blockdiag_out(v):
        blk = v.transpose(0, 2, 1).reshape(2, hg, P, C)
        return jnp.einsum('hgpc,gk->hgpkc', blk, eye).reshape(2, hg * P, hg * C).astype(BF16)

    bre = blockdiag_in(jnp.real(b_bar))
    bim = blockdiag_in(jnp.imag(b_bar))
    cre = blockdiag_out(c_re.astype(F32))
    cim = blockdiag_out(-c_im.astype(F32))
    lre = jnp.broadcast_to(jnp.real(lam_bar).reshape(1, G * P), (batch, G * P))
    lim = jnp.broadcast_to(jnp.imag(lam_bar).reshape(1, G * P), (batch, G * P))
    ublk = u_wide3.shape[2] // S5_DIM - 1
    const2 = lambda i: (0, 0)
    const3 = lambda i: (0, 0, 0)
    return pl.pallas_call(
        _s5_kernel,
        grid=(seq // ts,),
        in_specs=[
            pl.BlockSpec((batch, ts, S5_DIM), lambda i: (0, i, ublk)),
            pl.BlockSpec((2, hg * C, hg * P), const3),
            pl.BlockSpec((2, hg * C, hg * P), const3),
            pl.BlockSpec((2, hg * P, hg * C), const3),
            pl.BlockSpec((2, hg * P, hg * C), const3),
            pl.BlockSpec((batch, G * P), const2),
            pl.BlockSpec((batch, G * P), const2),
            pl.BlockSpec((1, S5_DIM), const2),
            pl.BlockSpec((S5_DIM, S5_DIM), const2),
            pl.BlockSpec((1, S5_DIM), const2),
        ],
        out_specs=pl.BlockSpec((batch, ts, S5_DIM), lambda i: (0, i, 0)),
        out_shape=jax.ShapeDtypeStruct((batch, seq, S5_DIM), ACT_DTYPE),
        scratch_shapes=[pltpu.VMEM((rows, S5_LANES), F32), pltpu.VMEM((rows, S5_LANES), F32),
                        pltpu.VMEM((2, batch, S5_LANES), F32),
                        pltpu.VMEM((S5_DIM // LANES, rows, LANES), F32)],
        compiler_params=_cparams("arbitrary"),
        name="s5_mixer",
    )(u_wide3, bre, bim, cre, cim, lre, lim, d_skip.astype(F32).reshape(1, S5_DIM),
      glu_w.astype(BF16), glu_b.astype(F32).reshape(1, S5_DIM))


def _ret_kernel(q_ref, k_ref, v_ref, cos_ref, sin_ref, dmat_ref, qdec_ref, kdec_ref, cg_ref, o_ref, state_ref):
    L = RET_CHUNK
    dk, dv = RET_KEY_DIM, RET_VAL_DIM
    half = dk // 2

    @pl.when(pl.program_id(1) == 0)
    def _():
        state_ref[...] = jnp.zeros(state_ref.shape, F32)

    def rot(t, cos, sin):
        t1, t2 = t[:, :half], t[:, half:]
        return jnp.concatenate([t1 * cos - t2 * sin, t1 * sin + t2 * cos], axis=1)

    for ci in range(q_ref.shape[0] // L):
        rs = slice(ci * L, (ci + 1) * L)
        cos = cos_ref[rs, :]
        sin = sin_ref[rs, :]
        for h in range(RET_HEADS):
            q = rot(q_ref[rs, h * dk:(h + 1) * dk].astype(F32), cos, sin)
            k = rot(k_ref[rs, h * dk:(h + 1) * dk].astype(F32), cos, sin) * (dk ** -0.5)
            v = v_ref[rs, h * dv:(h + 1) * dv]
            q16 = q.astype(BF16)
            k_t = k.T
            scores = _dot(q16, k_t.astype(BF16)) * dmat_ref[h]
            prev = state_ref[h]
            o = _dot(scores.astype(BF16), v) + _dot(q16, prev.astype(BF16)) * qdec_ref[:, h:h + 1]
            kv = _dot((k_t * kdec_ref[h:h + 1, :]).astype(BF16), v)
            state_ref[h] = prev * cg_ref[h:h + 1, :] + kv
            o_ref[rs, h * dv:(h + 1) * dv] = o.astype(o_ref.dtype)


def _ret_mixer(u_wide, batch, seq):
    L = RET_CHUNK
    m = batch * seq
    inv_freq = ROPE_BASE ** (-jnp.arange(0, RET_KEY_DIM, 2, dtype=F32) / RET_KEY_DIM)
    ang = jnp.arange(seq, dtype=F32)[:, None] * inv_freq[None, :]
    cos, sin = jnp.cos(ang), jnp.sin(ang)
    gamma = 1.0 - jnp.exp(jnp.linspace(math.log(1.0 / 32), math.log(1.0 / 512), RET_HEADS, dtype=F32))
    log_g = jnp.log(gamma)
    idx = jnp.arange(L, dtype=F32)
    diff = idx[:, None] - idx[None, :]
    dmat = jnp.where(diff >= 0, jnp.exp(log_g[:, None, None] * jnp.maximum(diff, 0.0)), 0.0)
    kdec = jnp.exp(log_g[:, None] * (L - 1.0 - idx)[None, :])
    kdec = jnp.pad(kdec, ((0, SUBLANES - RET_HEADS), (0, 0)))
    qdec = jnp.exp(log_g[None, :] * (idx + 1.0)[:, None])
    qdec = jnp.pad(qdec, ((0, 0), (0, LANES - RET_HEADS)))
    cg = jnp.broadcast_to(jnp.exp(log_g * L)[:, None], (RET_HEADS, RET_VAL_DIM))
    cg = jnp.pad(cg, ((0, SUBLANES - RET_HEADS), (0, 0)))
    T = RET_STEP_CHUNKS * L
    ns = seq // T
    row = lambda b, c: b * ns + c
    return pl.pallas_call(
        _ret_kernel,
        grid=(batch, ns),
        in_specs=[
            pl.BlockSpec((T, RET_QK), lambda b, c: (row(b, c), 0)),
            pl.BlockSpec((T, RET_QK), lambda b, c: (row(b, c), 1)),
            pl.BlockSpec((T, RET_V), lambda b, c: (row(b, c), 2)),
            pl.BlockSpec((T, RET_KEY_DIM // 2), lambda b, c: (c, 0)),
            pl.BlockSpec((T, RET_KEY_DIM // 2), lambda b, c: (c, 0)),
            pl.BlockSpec((RET_HEADS, L, L), lambda b, c: (0, 0, 0)),
            pl.BlockSpec((L, LANES), lambda b, c: (0, 0)),
            pl.BlockSpec((SUBLANES, L), lambda b, c: (0, 0)),
            pl.BlockSpec((SUBLANES, RET_VAL_DIM), lambda b, c: (0, 0)),
        ],
        out_specs=pl.BlockSpec((T, RET_V), lambda b, c: (row(b, c), 0)),
        out_shape=jax.ShapeDtypeStruct((m, RET_V), ACT_DTYPE),
        scratch_shapes=[pltpu.VMEM((RET_HEADS, RET_KEY_DIM, RET_VAL_DIM), F32)],
        compiler_params=_cparams("parallel", "arbitrary"),
        name="retention_mixer",
    )(u_wide, u_wide, u_wide, cos, sin, dmat, qdec, kdec, cg)


def kernel(x, ev_norm, ev_w_in, ev_conv_w, ev_conv_b, ev_dt_bias, ev_a_log, ev_d, ev_ssd_norm, ev_cf_conv_w, ev_cf_conv_b, ev_cf_ln_g, ev_cf_ln_b, ev_w_out, od_norm, od_w_in, od_lam_re, od_lam_im, od_b_re, od_b_im, od_c_re, od_c_im, od_log_dt, od_s5_d, od_glu_w, od_glu_b, od_gn_g, od_gn_b, od_w_out, ffn_norm, ffn_w_gate, ffn_w_up, ffn_w_down, final_norm):
    batch, seq, d = x.shape
    m = batch * seq
    xf = x.reshape(m, d)

    w = ev_w_in[0]
    o1 = SSD_INNER
    o2 = o1 + SSD_XBC
    o3 = o2 + SSD_HEADS
    w_main = jnp.concatenate([w[:, o1:o2], w[:, o3:], w[:, :o1]], axis=1).astype(BF16)
    w_dt = jnp.pad(w[:, o2:o3], ((0, 0), (0, LANES - SSD_HEADS))).astype(BF16)
    u_wide, dt_raw = _norm_proj(xf, ev_norm[0], w_main, w_dt, name="even_in_proj")
    ya = _ssd_mixer(u_wide, dt_raw, ev_conv_w[0], ev_conv_b[0], ev_dt_bias[0], ev_a_log[0], ev_d[0], batch, seq)
    yb = _conf_mixer(u_wide, ev_cf_conv_w[0], ev_cf_conv_b[0], batch, seq)
    z_block = (SSD_XBC + 2 * CONF_DIM) // SSD_INNER
    xf = _mix_ffn(xf, ya, ("gate_rms", (u_wide, z_block), ev_ssd_norm[0], None),
                  yb, ("ln_swish", None, ev_cf_ln_g[0], ev_cf_ln_b[0]),
                  ev_w_out[0], ffn_norm[0], ffn_w_gate[0], ffn_w_up[0], ffn_w_down[0], None, "even_out_ffn")

    w = od_w_in[0]
    w_main = jnp.concatenate([w[:, S5_DIM:], w[:, :S5_DIM]], axis=1).astype(BF16)
    (u_wide,) = _norm_proj(xf, od_norm[0], w_main, None, name="odd_in_proj")
    yc = _s5_mixer(u_wide.reshape(batch, seq, -1), od_lam_re[0], od_lam_im[0], od_b_re[0], od_b_im[0],
                   od_c_re[0], od_c_im[0], od_log_dt[0], od_s5_d[0], od_glu_w[0], od_glu_b[0], batch, seq)
    yd = _ret_mixer(u_wide, batch, seq)
    gate_block = (2 * RET_QK + RET_V) // RET_V
    xf = _mix_ffn(xf, yc.reshape(m, S5_DIM), None,
                  yd, ("gn_gate", (u_wide, gate_block), od_gn_g[0], od_gn_b[0]),
                  od_w_out[0], ffn_norm[1], ffn_w_gate[1], ffn_w_up[1], ffn_w_down[1], final_norm, "odd_out_ffn")
    return xf.reshape(batch, seq, d)
```

```python
import functools
import math

import numpy as np
import jax
import jax.numpy as jnp
from jax import lax
from jax.experimental import pallas as pl
from jax.experimental.pallas import tpu as pltpu

F32 = jnp.float32
BF16 = jnp.bfloat16
ACT_DTYPE = BF16

D_MODEL = 1024
EPS = 1e-5

SSD_HEADS = 16
SSD_HEAD_DIM = 64
SSD_INNER = SSD_HEADS * SSD_HEAD_DIM
SSD_GROUPS = 4
SSD_STATE = 128
SSD_CONV = 4
SSD_CHUNK = 128
SSD_XBC = SSD_INNER + 2 * SSD_GROUPS * SSD_STATE

CONF_DIM = 1024
CONF_KERNEL = 31

S5_DIM = 512
S5_GROUP = 16
S5_GROUPS = S5_DIM // S5_GROUP
S5_STATE = 64
S5_LANES = S5_GROUPS * S5_STATE
S5_HALF = S5_LANES // 2

RET_HEADS = 4
RET_KEY_DIM = 256
RET_VAL_DIM = 256
RET_CHUNK = 128
RET_QK = RET_HEADS * RET_KEY_DIM
RET_V = RET_HEADS * RET_VAL_DIM
ROPE_BASE = 10000.0

LANES = 128
SUBLANES = 8
VMEM_LIMIT_BYTES = 56 * 1024 * 1024

PROJ_ROWS = 1024
PROJ_COLS = 512
FFN_ROWS = 512
FFN_COLS = 256
CONF_ROWS = 512
CONF_HALO = 32
S5_STEPS = 128
RET_STEP_CHUNKS = 8
SSD_STEP_CHUNKS = 8


def _cparams(*sem):
    return pltpu.CompilerParams(dimension_semantics=sem, vmem_limit_bytes=VMEM_LIMIT_BYTES)


def _dot(a, b):
    return jnp.dot(a, b, preferred_element_type=F32)


def _split3(x):
    x1 = x.astype(BF16)
    r1 = x - x1.astype(F32)
    x2 = r1.astype(BF16)
    x3 = (r1 - x2.astype(F32)).astype(BF16)
    return x1, x2, x3


def _rms(x, g):
    return x * lax.rsqrt(jnp.mean(x * x, axis=-1, keepdims=True) + EPS) * g


def _norm_proj_kernel(*refs, has_aux):
    if has_aux:
        x_ref, g_ref, w_ref, waux_ref, o_ref, oaux_ref = refs
    else:
        x_ref, g_ref, w_ref, o_ref = refs
    h = _rms(x_ref[...], g_ref[...]).astype(BF16)
    for k in range(w_ref.shape[1] // PROJ_COLS):
        sl = slice(k * PROJ_COLS, (k + 1) * PROJ_COLS)
        o_ref[:, sl] = _dot(h, w_ref[:, sl]).astype(o_ref.dtype)
    if has_aux:
        oaux_ref[...] = _dot(h, waux_ref[...])


def _norm_proj(x, g, w, w_aux, *, name):
    m, d = x.shape
    n = w.shape[1]
    tm = PROJ_ROWS
    has_aux = w_aux is not None
    resident = dict(pipeline_mode=pl.Buffered(1))
    in_specs = [pl.BlockSpec((tm, d), lambda i: (i, 0)),
                pl.BlockSpec((1, d), lambda i: (0, 0)),
                pl.BlockSpec((d, n), lambda i: (0, 0), **resident)]
    out_specs = [pl.BlockSpec((tm, n), lambda i: (i, 0))]
    out_shape = [jax.ShapeDtypeStruct((m, n), ACT_DTYPE)]
    args = [x, g.reshape(1, d), w]
    if has_aux:
        in_specs.append(pl.BlockSpec((d, LANES), lambda i: (0, 0), **resident))
        out_specs.append(pl.BlockSpec((tm, LANES), lambda i: (i, 0)))
        out_shape.append(jax.ShapeDtypeStruct((m, LANES), F32))
        args.append(w_aux)
    return pl.pallas_call(
        functools.partial(_norm_proj_kernel, has_aux=has_aux),
        grid=(m // tm,),
        in_specs=in_specs, out_specs=out_specs, out_shape=out_shape,
        compiler_params=_cparams("parallel"),
        name=name,
    )(*args)


def _ssd_kernel(xbc_ref, dt_ref, cw_ref, cb_ref, dtb_ref, alog_ref, dskip_ref, e_ref, o_ref, tail_ref, state_ref):
    L = SSD_CHUNK
    N = SSD_STATE
    gw = SSD_INNER // SSD_GROUPS

    @pl.when(pl.program_id(1) == 0)
    def _():
        tail_ref[...] = jnp.zeros(tail_ref.shape, tail_ref.dtype)
        state_ref[...] = jnp.zeros(state_ref.shape, F32)

    lane = lax.broadcasted_iota(jnp.int32, (L, LANES), 1)
    low = lane < SSD_HEAD_DIM
    row = lax.broadcasted_iota(jnp.int32, (L, L), 0)
    col = lax.broadcasted_iota(jnp.int32, (L, L), 1)
    tril = row >= col
    tri = tril.astype(BF16)

    def per_head_to_lanes(v):
        v1 = v.astype(BF16)
        v2 = (v - v1.astype(F32)).astype(BF16)
        return _dot(v1, e_ref[...]) + _dot(v2, e_ref[...])

    for ci in range(xbc_ref.shape[0] // L):
        rs = slice(ci * L, (ci + 1) * L)
        x_in = xbc_ref[rs, :].astype(F32)
        ext = jnp.concatenate([tail_ref[...], x_in], axis=0)
        tail_ref[...] = x_in[L - SUBLANES:, :]
        acc = cb_ref[...] + cw_ref[SSD_CONV - 1:SSD_CONV, :] * x_in
        for d in range(1, SSD_CONV):
            acc = acc + cw_ref[SSD_CONV - 1 - d:SSD_CONV - d, :] * ext[SUBLANES - d:SUBLANES - d + L]
        xbc = acc * jax.nn.sigmoid(acc)
        xs = xbc[:, :SSD_INNER]
        bm = xbc[:, SSD_INNER:SSD_INNER + SSD_GROUPS * N]
        cm = xbc[:, SSD_INNER + SSD_GROUPS * N:]

        dt = jax.nn.softplus(dt_ref[rs, :] + dtb_ref[...])
        a = jnp.where(lane < SSD_HEADS, dt * (-jnp.exp(alog_ref[...])), 0.0)
        a1, a2, a3 = _split3(a)
        a_cs = _dot(tri, a1) + _dot(tri, a2) + _dot(tri, a3)
        a_cs_t = a_cs.T
        ea = jnp.exp(a_cs)
        ds = jnp.exp(a_cs[L - 1:L, :] - a_cs)
        dt_x = per_head_to_lanes(dt)
        dtds_x = per_head_to_lanes(dt * ds)
        ea_x = per_head_to_lanes(ea)

        y_tiles = []
        for g in range(SSD_GROUPS):
            bg = bm[:, g * N:(g + 1) * N]
            cg = cm[:, g * N:(g + 1) * N].astype(BF16)
            bg_t = bg.T.astype(BF16)
            cb = _dot(cg, bg_t)
            prev = state_ref[g]
            y_off = _dot(cg, prev.astype(BF16))
            xd_tiles = []
            for jj in range(2):
                j = 2 * g + jj
                ps = slice(j * LANES, (j + 1) * LANES)
                xs_p = xs[:, ps]
                xdt_p = xs_p * dt_x[:, ps]
                scores = []
                for h in (2 * j, 2 * j + 1):
                    diff = a_cs[:, h:h + 1] - a_cs_t[h:h + 1, :]
                    lmat = jnp.exp(jnp.where(tril, diff, -jnp.inf))
                    scores.append((cb * lmat).astype(BF16))
                lhs = jnp.concatenate(scores, axis=1)
                rhs = jnp.concatenate([jnp.where(low, xdt_p, 0.0), jnp.where(low, 0.0, xdt_p)],
                                      axis=0).astype(BF16)
                y_diag = _dot(lhs, rhs)
                y_tiles.append(y_diag + y_off[:, jj * LANES:(jj + 1) * LANES] * ea_x[:, ps]
                               + dskip_ref[:, ps] * xs_p)
                xd_tiles.append((xs_p * dtds_x[:, ps]).astype(BF16))
            xd = jnp.concatenate(xd_tiles, axis=1)
            decay = ea_x[L - 1:L, g * gw:(g + 1) * gw]
            state_ref[g] = prev * decay + _dot(bg_t, xd)

        o_ref[rs, :] = jnp.concatenate(y_tiles, axis=1).astype(o_ref.dtype)


def _ssd_mixer(u_wide, dt_raw, conv_w, conv_b, dt_bias, a_log, d_skip, batch, seq):
    L = SSD_CHUNK
    T = SSD_STEP_CHUNKS * L
    ns = seq // T
    m = batch * seq

    def pad_heads(v):
        return jnp.pad(v.astype(F32), (0, LANES - SSD_HEADS)).reshape(1, LANES)

    dskip = jnp.repeat(d_skip.astype(F32), SSD_HEAD_DIM).reshape(1, SSD_INNER)
    e_np = np.zeros((LANES, SSD_INNER), np.float32)
    for hh in range(SSD_HEADS):
        e_np[hh, hh * SSD_HEAD_DIM:(hh + 1) * SSD_HEAD_DIM] = 1.0
    head_to_lanes = jnp.asarray(e_np, dtype=BF16)
    return pl.pallas_call(
        _ssd_kernel,
        grid=(batch, ns),
        in_specs=[
            pl.BlockSpec((T, SSD_XBC), lambda b, c: (b * ns + c, 0)),
            pl.BlockSpec((T, LANES), lambda b, c: (b * ns + c, 0)),
            pl.BlockSpec((SSD_CONV, SSD_XBC), lambda b, c: (0, 0)),
            pl.BlockSpec((1, SSD_XBC), lambda b, c: (0, 0)),
            pl.BlockSpec((1, LANES), lambda b, c: (0, 0)),
            pl.BlockSpec((1, LANES), lambda b, c: (0, 0)),
            pl.BlockSpec((1, SSD_INNER), lambda b, c: (0, 0)),
            pl.BlockSpec((LANES, SSD_INNER), lambda b, c: (0, 0)),
        ],
        out_specs=pl.BlockSpec((T, SSD_INNER), lambda b, c: (b * ns + c, 0)),
        out_shape=jax.ShapeDtypeStruct((m, SSD_INNER), ACT_DTYPE),
        scratch_shapes=[pltpu.VMEM((SUBLANES, SSD_XBC), F32),
                        pltpu.VMEM((SSD_GROUPS, SSD_STATE, SSD_INNER // SSD_GROUPS), F32)],
        compiler_params=_cparams("parallel", "arbitrary"),
        name="ssd_mixer",
    )(u_wide, dt_raw, conv_w.astype(F32), conv_b.astype(F32).reshape(1, SSD_XBC),
      pad_heads(dt_bias), pad_heads(a_log), dskip, head_to_lanes)


def _conf_kernel(ag_ref, w_ref, b_ref, o_ref, sh_ref, y_ref):
    T = CONF_ROWS
    H = CONF_HALO
    strips = CONF_DIM // LANES

    @pl.when(pl.program_id(1) == 0)
    def _():
        sh_ref[...] = jnp.zeros(sh_ref.shape, F32)

    a = ag_ref[:, :CONF_DIM].astype(F32)
    gate = ag_ref[:, CONF_DIM:].astype(F32)
    h = a * jax.nn.sigmoid(gate)
    for s in range(SUBLANES):
        for c in range(strips):
            sh_ref[s, c, 0:H, :] = sh_ref[s, c, T:T + H, :]
            sh_ref[s, c, pl.ds(H - s, T), :] = h[:, c * LANES:(c + 1) * LANES]

    base = H - (CONF_KERNEL - 1)
    nt = 32
    groups = T // (nt * SUBLANES)

    def strip_group(gi, carry):
        c = gi // groups
        r0 = pl.multiple_of((gi % groups) * (nt * SUBLANES), nt * SUBLANES)
        tiles = [jnp.broadcast_to(b_ref[c], (SUBLANES, LANES))] * nt
        for j in range(CONF_KERNEL):
            q, s = divmod(base + j, SUBLANES)
            wj = jnp.broadcast_to(w_ref[c, j:j + 1, :], (SUBLANES, LANES))
            tiles = [t + wj * sh_ref[s, c, pl.ds(r0 + (q + i) * SUBLANES, SUBLANES), :]
                     for i, t in enumerate(tiles)]
        y_ref[c, pl.ds(r0, nt * SUBLANES), :] = jnp.concatenate(tiles, axis=0)
        return carry

    lax.fori_loop(0, strips * groups, strip_group, 0)

    for c in range(strips):
        o_ref[:, c * LANES:(c + 1) * LANES] = y_ref[c].astype(o_ref.dtype)


def _conf_mixer(u_wide, conv_w, conv_b, batch, seq):
    T = CONF_ROWS
    nt = seq // T
    m = batch * seq
    strips = CONF_DIM // LANES
    taps_pad = CONF_KERNEL + 1
    w_strips = jnp.pad(conv_w.astype(F32), ((0, 1), (0, 0))).reshape(taps_pad, strips, LANES).transpose(1, 0, 2)
    b_strips = conv_b.astype(F32).reshape(strips, 1, LANES)
    return pl.pallas_call(
        _conf_kernel,
        grid=(batch, nt),
        in_specs=[
            pl.BlockSpec((T, 2 * CONF_DIM), lambda b, c: (b * nt + c, 1)),
            pl.BlockSpec((strips, taps_pad, LANES), lambda b, c: (0, 0, 0)),
            pl.BlockSpec((strips, 1, LANES), lambda b, c: (0, 0, 0)),
        ],
        out_specs=pl.BlockSpec((T, CONF_DIM), lambda b, c: (b * nt + c, 0)),
        out_shape=jax.ShapeDtypeStruct((m, CONF_DIM), ACT_DTYPE),
        scratch_shapes=[pltpu.VMEM((SUBLANES, CONF_DIM // LANES, T + CONF_HALO, LANES), F32),
                        pltpu.VMEM((CONF_DIM // LANES, T, LANES), F32)],
        compiler_params=_cparams("parallel", "arbitrary"),
        name="conformer_mixer",
    )(u_wide, w_strips, b_strips)


def _finish_mixer(v_ref, aux_ref, gain_ref, bias_ref, kind):
    if kind is None:
        return v_ref[...]
    v = v_ref[...].astype(F32)
    width = 2 * LANES
    if kind == "ln_swish":
        mu = jnp.mean(v, axis=-1, keepdims=True)
        var = jnp.mean(jnp.square(v - mu), axis=-1, keepdims=True)
        hn = (v - mu) * lax.rsqrt(var + EPS) * gain_ref[...] + bias_ref[...]
        return (hn * jax.nn.sigmoid(hn)).astype(BF16)
    aux = aux_ref[...].astype(F32)
    gate = aux * jax.nn.sigmoid(aux)
    outs = []
    for gi in range(v.shape[1] // width):
        sl = slice(gi * width, (gi + 1) * width)
        if kind == "gate_rms":
            vg = v[:, sl] * gate[:, sl]
            outs.append(vg * lax.rsqrt(jnp.mean(vg * vg, axis=-1, keepdims=True) + EPS) * gain_ref[:, sl])
        else:
            vg = v[:, sl]
            mu = jnp.mean(vg, axis=-1, keepdims=True)
            var = jnp.mean(jnp.square(vg - mu), axis=-1, keepdims=True)
            on = (vg - mu) * lax.rsqrt(var + EPS) * gain_ref[:, sl] + bias_ref[:, sl]
            outs.append(gate[:, sl] * on)
    return jnp.concatenate(outs, axis=1).astype(BF16)


def _mix_ffn_kernel(*refs, a_kind, b_kind, final_norm):
    refs = list(refs)
    x_ref, a_ref, b_ref = refs[:3]
    del refs[:3]
    a_aux = refs.pop(0) if a_kind in ("gate_rms", "gn_gate") else None
    b_aux = refs.pop(0) if b_kind in ("gate_rms", "gn_gate") else None
    ag_ref, ab_ref, bg_ref, bb_ref, wa_ref, wb_ref, g_ref, wg_ref, wu_ref, wd_ref, fg_ref, o_ref, act_ref = refs
    a = _finish_mixer(a_ref, a_aux, ag_ref, ab_ref, a_kind)
    b = _finish_mixer(b_ref, b_aux, bg_ref, bb_ref, b_kind)
    x = x_ref[...] + _dot(a, wa_ref[...]) + _dot(b, wb_ref[...])
    h = _rms(x, g_ref[...]).astype(BF16)
    for k in range(wg_ref.shape[1] // FFN_COLS):
        sl = slice(k * FFN_COLS, (k + 1) * FFN_COLS)
        gate = _dot(h, wg_ref[:, sl])
        up = _dot(h, wu_ref[:, sl])
        act_ref[:, sl] = (gate * jax.nn.sigmoid(gate) * up).astype(BF16)
    y = x + _dot(act_ref[...], wd_ref[...])
    if final_norm:
        y = _rms(y, fg_ref[...])
    o_ref[...] = y


def _mix_ffn(x, a, a_tail, b, b_tail, w_out, g, w_gate, w_up, w_down, final_g, name):
    m, d = x.shape
    ka, kb = a.shape[1], b.shape[1]
    f = w_gate.shape[1]
    tm = FFN_ROWS
    final_norm = final_g is not None
    fg = (final_g if final_norm else jnp.ones((d,), F32)).astype(F32).reshape(1, d)
    resident = dict(pipeline_mode=pl.Buffered(1))
    row_block = lambda k: pl.BlockSpec((tm, k), lambda i: (i, 0))
    vec_block = lambda k: pl.BlockSpec((1, k), lambda i: (0, 0))
    args = [x, a, b]
    in_specs = [row_block(d), row_block(ka), row_block(kb)]
    kinds, vecs = [], []
    for tail, k in ((a_tail, ka), (b_tail, kb)):
        kind, aux, gain, bias = tail if tail is not None else (None, None, None, None)
        kinds.append(kind)
        if aux is not None:
            arr, blk = aux
            args.append(arr)
            in_specs.append(pl.BlockSpec((tm, k), lambda i, blk=blk: (i, blk)))
        gain = jnp.ones((k,), F32) if gain is None else gain
        bias = jnp.zeros((k,), F32) if bias is None else bias
        vecs += [gain.astype(F32).reshape(1, k), bias.astype(F32).reshape(1, k)]
    args += vecs + [w_out[:ka].astype(BF16), w_out[ka:].astype(BF16), g.astype(F32).reshape(1, d),
                    w_gate.astype(BF16), w_up.astype(BF16), w_down.astype(BF16), fg]
    in_specs += [vec_block(ka), vec_block(ka), vec_block(kb), vec_block(kb),
                 pl.BlockSpec((ka, d), lambda i: (0, 0), **resident),
                 pl.BlockSpec((kb, d), lambda i: (0, 0), **resident),
                 vec_block(d),
                 pl.BlockSpec((d, f), lambda i: (0, 0), **resident),
                 pl.BlockSpec((d, f), lambda i: (0, 0), **resident),
                 pl.BlockSpec((f, d), lambda i: (0, 0), **resident),
                 vec_block(d)]
    return pl.pallas_call(
        functools.partial(_mix_ffn_kernel, a_kind=kinds[0], b_kind=kinds[1], final_norm=final_norm),
        grid=(m // tm,),
        in_specs=in_specs,
        out_specs=pl.BlockSpec((tm, d), lambda i: (i, 0)),
        out_shape=jax.ShapeDtypeStruct((m, d), F32),
        scratch_shapes=[pltpu.VMEM((tm, f), BF16)],
        compiler_params=_cparams("parallel"),
        name=name,
    )(*args)


def _s5_kernel(u_ref, bre_ref, bim_ref, cre_ref, cim_ref, lre_ref, lim_ref, d_ref,
               gw_ref, gb_ref, o_ref, sre_ref, sim_ref, st_ref, rl_ref):
    nb, ts, dim = u_ref.shape
    rows = nb * ts
    half_in = dim // 2

    @pl.when(pl.program_id(0) == 0)
    def _():
        st_ref[...] = jnp.zeros(st_ref.shape, F32)

    for b in range(nb):
        ub = u_ref[b].astype(F32)
        for k in range(dim // LANES):
            rl_ref[k, pl.ds(b, ts, stride=nb), :] = ub[:, k * LANES:(k + 1) * LANES]
    u_tm = jnp.concatenate([rl_ref[k] for k in range(dim // LANES)], axis=1)
    u16 = u_tm.astype(BF16)

    for hf in range(2):
        uh = u16[:, hf * half_in:(hf + 1) * half_in]
        sre_ref[:, hf * S5_HALF:(hf + 1) * S5_HALF] = _dot(uh, bre_ref[hf])
        sim_ref[:, hf * S5_HALF:(hf + 1) * S5_HALF] = _dot(uh, bim_ref[hf])

    for hf in range(2):
        sl = slice(hf * S5_HALF, (hf + 1) * S5_HALF)
        lr = lre_ref[:, sl]
        li = lim_ref[:, sl]

        def step(t, carry, sl=sl, lr=lr, li=li):
            sr, si = carry
            r0 = pl.multiple_of(t * nb, nb)
            nr = lr * sr - li * si + sre_ref[pl.ds(r0, nb), sl]
            ni = lr * si + li * sr + sim_ref[pl.ds(r0, nb), sl]
            sre_ref[pl.ds(r0, nb), sl] = nr
            sim_ref[pl.ds(r0, nb), sl] = ni
            return nr, ni

        sr, si = lax.fori_loop(0, ts, step, (st_ref[0, :, sl], st_ref[1, :, sl]), unroll=True)
        st_ref[0, :, sl] = sr
        st_ref[1, :, sl] = si

    ys = []
    for hf in range(2):
        sl = slice(hf * S5_HALF, (hf + 1) * S5_HALF)
        ys.append(_dot(sre_ref[:, sl].astype(BF16), cre_ref[hf]) + _dot(sim_ref[:, sl].astype(BF16), cim_ref[hf]))
    y = jnp.concatenate(ys, axis=1) + d_ref[...] * u_tm
    y = jax.nn.gelu(y)
    y = y * jax.nn.sigmoid(_dot(y.astype(BF16), gw_ref[...]) + gb_ref[...])
    for k in range(dim // LANES):
        rl_ref[k] = y[:, k * LANES:(k + 1) * LANES]
    for b in range(nb):
        for k in range(dim // LANES):
            o_ref[b, :, k * LANES:(k + 1) * LANES] = rl_ref[k, pl.ds(b, ts, stride=nb), :].astype(o_ref.dtype)


def _s5_mixer(u_wide3, lam_re, lam_im, b_re, b_im, c_re, c_im, log_dt, d_skip, glu_w, glu_b, batch, seq):
    ts = S5_STEPS
    rows = batch * ts
    G, P, C = S5_GROUPS, S5_STATE, S5_GROUP
    hg = G // 2
    lam = lax.complex(lam_re.astype(F32), lam_im.astype(F32))
    dt = jnp.exp(log_dt.astype(F32))[:, None]
    lam_bar = jnp.exp(lam * dt)
    b_bar = ((lam_bar - 1.0) / lam)[..., None] * lax.complex(b_re.astype(F32), b_im.astype(F32))
    eye = jnp.eye(hg, dtype=F32)

    def blockdiag_in(v):
        blk = v.transpose(0, 2, 1).reshape(2, hg, C, P)
        return jnp.einsum('hgcp,gk->hgckp', blk, eye).reshape(2, hg * C, hg * P).astype(BF16)

    def blockdiag_out(v):
        blk = v.transpose(0, 2, 1).reshape(2, hg, P, C)
        return jnp.einsum('hgpc,gk->hgpkc', blk, eye).reshape(2, hg * P, hg * C).astype(BF16)

    bre = blockdiag_in(jnp.real(b_bar))
    bim = blockdiag_in(jnp.imag(b_bar))
    cre = blockdiag_out(c_re.astype(F32))
    cim = blockdiag_out(-c_im.astype(F32))
    lre = jnp.broadcast_to(jnp.real(lam_bar).reshape(1, G * P), (batch, G * P))
    lim = jnp.broadcast_to(jnp.imag(lam_bar).reshape(1, G * P), (batch, G * P))
    ublk = u_wide3.shape[2] // S5_DIM - 1
    const2 = lambda i: (0, 0)
    const3 = lambda i: (0, 0, 0)
    return pl.pallas_call(
        _s5_kernel,
        grid=(seq // ts,),
        in_specs=[
            pl.BlockSpec((batch, ts, S5_DIM), lambda i: (0, i, ublk)),
            pl.BlockSpec((2, hg * C, hg * P), const3),
            pl.BlockSpec((2, hg * C, hg * P), const3),
            pl.BlockSpec((2, hg * P, hg * C), const3),
            pl.BlockSpec((2, hg * P, hg * C), const3),
            pl.BlockSpec((batch, G * P), const2),
            pl.BlockSpec((batch, G * P), const2),
            pl.BlockSpec((1, S5_DIM), const2),
            pl.BlockSpec((S5_DIM, S5_DIM), const2),
            pl.BlockSpec((1, S5_DIM), const2),
        ],
        out_specs=pl.BlockSpec((batch, ts, S5_DIM), lambda i: (0, i, 0)),
        out_shape=jax.ShapeDtypeStruct((batch, seq, S5_DIM), ACT_DTYPE),
        scratch_shapes=[pltpu.VMEM((rows, S5_LANES), F32), pltpu.VMEM((rows, S5_LANES), F32),
                        pltpu.VMEM((2, batch, S5_LANES), F32),
                        pltpu.VMEM((S5_DIM // LANES, rows, LANES), F32)],
        compiler_params=_cparams("arbitrary"),
        name="s5_mixer",
    )(u_wide3, bre, bim, cre, cim, lre, lim, d_skip.astype(F32).reshape(1, S5_DIM),
      glu_w.astype(BF16), glu_b.astype(F32).reshape(1, S5_DIM))


def _ret_kernel(q_ref, k_ref, v_ref, cos_ref, sin_ref, dmat_ref, qdec_ref, kdec_ref, cg_ref, o_ref, state_ref):
    L = RET_CHUNK
    dk, dv = RET_KEY_DIM, RET_VAL_DIM
    half = dk // 2

    @pl.when(pl.program_id(1) == 0)
    def _():
        state_ref[...] = jnp.zeros(state_ref.shape, F32)

    def rot(t, cos, sin):
        t1, t2 = t[:, :half], t[:, half:]
        return jnp.concatenate([t1 * cos - t2 * sin, t1 * sin + t2 * cos], axis=1)

    for ci in range(q_ref.shape[0] // L):
        rs = slice(ci * L, (ci + 1) * L)
        cos = cos_ref[rs, :]
        sin = sin_ref[rs, :]
        for h in range(RET_HEADS):
            q = rot(q_ref[rs, h * dk:(h + 1) * dk].astype(F32), cos, sin)
            k = rot(k_ref[rs, h * dk:(h + 1) * dk].astype(F32), cos, sin) * (dk ** -0.5)
            v = v_ref[rs, h * dv:(h + 1) * dv]
            q16 = q.astype(BF16)
            k_t = k.T
            scores = _dot(q16, k_t.astype(BF16)) * dmat_ref[h]
            prev = state_ref[h]
            o = _dot(scores.astype(BF16), v) + _dot(q16, prev.astype(BF16)) * qdec_ref[:, h:h + 1]
            kv = _dot((k_t * kdec_ref[h:h + 1, :]).astype(BF16), v)
            state_ref[h] = prev * cg_ref[h:h + 1, :] + kv
            o_ref[rs, h * dv:(h + 1) * dv] = o.astype(o_ref.dtype)


def _ret_mixer(u_wide, batch, seq):
    L = RET_CHUNK
    m = batch * seq
    inv_freq = ROPE_BASE ** (-jnp.arange(0, RET_KEY_DIM, 2, dtype=F32) / RET_KEY_DIM)
    ang = jnp.arange(seq, dtype=F32)[:, None] * inv_freq[None, :]
    cos, sin = jnp.cos(ang), jnp.sin(ang)
    gamma = 1.0 - jnp.exp(jnp.linspace(math.log(1.0 / 32), math.log(1.0 / 512), RET_HEADS, dtype=F32))
    log_g = jnp.log(gamma)
    idx = jnp.arange(L, dtype=F32)
    diff = idx[:, None] - idx[None, :]
    dmat = jnp.where(diff >= 0, jnp.exp(log_g[:, None, None] * jnp.maximum(diff, 0.0)), 0.0)
    kdec = jnp.exp(log_g[:, None] * (L - 1.0 - idx)[None, :])
    kdec = jnp.pad(kdec, ((0, SUBLANES - RET_HEADS), (0, 0)))
    qdec = jnp.exp(log_g[None, :] * (idx + 1.0)[:, None])
    qdec = jnp.pad(qdec, ((0, 0), (0, LANES - RET_HEADS)))
    cg = jnp.broadcast_to(jnp.exp(log_g * L)[:, None], (RET_HEADS, RET_VAL_DIM))
    cg = jnp.pad(cg, ((0, SUBLANES - RET_HEADS), (0, 0)))
    T = RET_STEP_CHUNKS * L
    ns = seq // T
    row = lambda b, c: b * ns + c
    return pl.pallas_call(
        _ret_kernel,
        grid=(batch, ns),
        in_specs=[
            pl.BlockSpec((T, RET_QK), lambda b, c: (row(b, c), 0)),
            pl.BlockSpec((T, RET_QK), lambda b, c: (row(b, c), 1)),
            pl.BlockSpec((T, RET_V), lambda b, c: (row(b, c), 2)),
            pl.BlockSpec((T, RET_KEY_DIM // 2), lambda b, c: (c, 0)),
            pl.BlockSpec((T, RET_KEY_DIM // 2), lambda b, c: (c, 0)),
            pl.BlockSpec((RET_HEADS, L, L), lambda b, c: (0, 0, 0)),
            pl.BlockSpec((L, LANES), lambda b, c: (0, 0)),
            pl.BlockSpec((SUBLANES, L), lambda b, c: (0, 0)),
            pl.BlockSpec((SUBLANES, RET_VAL_DIM), lambda b, c: (0, 0)),
        ],
        out_specs=pl.BlockSpec((T, RET_V), lambda b, c: (row(b, c), 0)),
        out_shape=jax.ShapeDtypeStruct((m, RET_V), ACT_DTYPE),
        scratch_shapes=[pltpu.VMEM((RET_HEADS, RET_KEY_DIM, RET_VAL_DIM), F32)],
        compiler_params=_cparams("parallel", "arbitrary"),
        name="retention_mixer",
    )(u_wide, u_wide, u_wide, cos, sin, dmat, qdec, kdec, cg)


def kernel(x, ev_norm, ev_w_in, ev_conv_w, ev_conv_b, ev_dt_bias, ev_a_log, ev_d, ev_ssd_norm, ev_cf_conv_w, ev_cf_conv_b, ev_cf_ln_g, ev_cf_ln_b, ev_w_out, od_norm, od_w_in, od_lam_re, od_lam_im, od_b_re, od_b_im, od_c_re, od_c_im, od_log_dt, od_s5_d, od_glu_w, od_glu_b, od_gn_g, od_gn_b, od_w_out, ffn_norm, ffn_w_gate, ffn_w_up, ffn_w_down, final_norm):
    batch, seq, d = x.shape
    m = batch * seq
    xf = x.reshape(m, d)

    w = ev_w_in[0]
    o1 = SSD_INNER
    o2 = o1 + SSD_XBC
    o3 = o2 + SSD_HEADS
    w_main = jnp.concatenate([w[:, o1:o2], w[:, o3:], w[:, :o1]], axis=1).astype(BF16)
    w_dt = jnp.pad(w[:, o2:o3], ((0, 0), (0, LANES - SSD_HEADS))).astype(BF16)
    u_wide, dt_raw = _norm_proj(xf, ev_norm[0], w_main, w_dt, name="even_in_proj")
    ya = _ssd_mixer(u_wide, dt_raw, ev_conv_w[0], ev_conv_b[0], ev_dt_bias[0], ev_a_log[0], ev_d[0], batch, seq)
    yb = _conf_mixer(u_wide, ev_cf_conv_w[0], ev_cf_conv_b[0], batch, seq)
    z_block = (SSD_XBC + 2 * CONF_DIM) // SSD_INNER
    xf = _mix_ffn(xf, ya, ("gate_rms", (u_wide, z_block), ev_ssd_norm[0], None),
                  yb, ("ln_swish", None, ev_cf_ln_g[0], ev_cf_ln_b[0]),
                  ev_w_out[0], ffn_norm[0], ffn_w_gate[0], ffn_w_up[0], ffn_w_down[0], None, "even_out_ffn")

    w = od_w_in[0]
    w_main = jnp.concatenate([w[:, S5_DIM:], w[:, :S5_DIM]], axis=1).astype(BF16)
    (u_wide,) = _norm_proj(xf, od_norm[0], w_main, None, name="odd_in_proj")
    yc = _s5_mixer(u_wide.reshape(batch, seq, -1), od_lam_re[0], od_lam_im[0], od_b_re[0], od_b_im[0],
                   od_c_re[0], od_c_im[0], od_log_dt[0], od_s5_d[0], od_glu_w[0], od_glu_b[0], batch, seq)
    yd = _ret_mixer(u_wide, batch, seq)
    gate_block = (2 * RET_QK + RET_V) // RET_V
    xf = _mix_ffn(xf, yc.reshape(m, S5_DIM), None,
                  yd, ("gn_gate", (u_wide, gate_block), od_gn_g[0], od_gn_b[0]),
                  od_w_out[0], ffn_norm[1], ffn_w_gate[1], ffn_w_up[1], ffn_w_down[1], final_norm, "odd_out_ffn")
    return xf.reshape(batch, seq, d)
```

```python
import functools
import math

import numpy as np
import jax
import jax.numpy as jnp
from jax import lax
from jax.experimental import pallas as pl
from jax.experimental.pallas import tpu as pltpu

F32 = jnp.float32
BF16 = jnp.bfloat16
ACT_DTYPE = BF16

D_MODEL = 1024
EPS = 1e-5

SSD_HEADS = 16
SSD_HEAD_DIM = 64
SSD_INNER = SSD_HEADS * SSD_HEAD_DIM
SSD_GROUPS = 4
SSD_STATE = 128
SSD_CONV = 4
SSD_CHUNK = 128
SSD_XBC = SSD_INNER + 2 * SSD_GROUPS * SSD_STATE

CONF_DIM = 1024
CONF_KERNEL = 31

S5_DIM = 512
S5_GROUP = 16
S5_GROUPS = S5_DIM // S5_GROUP
S5_STATE = 64
S5_LANES = S5_GROUPS * S5_STATE
S5_HALF = S5_LANES // 2

RET_HEADS = 4
RET_KEY_DIM = 256
RET_VAL_DIM = 256
RET_CHUNK = 128
RET_QK = RET_HEADS * RET_KEY_DIM
RET_V = RET_HEADS * RET_VAL_DIM
ROPE_BASE = 10000.0

LANES = 128
SUBLANES = 8
VMEM_LIMIT_BYTES = 56 * 1024 * 1024

PROJ_ROWS = 1024
PROJ_COLS = 512
FFN_ROWS = 512
FFN_COLS = 256
CONF_ROWS = 512
CONF_HALO = 32
S5_STEPS = 128
RET_STEP_CHUNKS = 8
SSD_STEP_CHUNKS = 8


def _cparams(*sem):
    return pltpu.CompilerParams(dimension_semantics=sem, vmem_limit_bytes=VMEM_LIMIT_BYTES)


def _dot(a, b):
    return jnp.dot(a, b, preferred_element_type=F32)


def _split3(x):
    x1 = x.astype(BF16)
    r1 = x - x1.astype(F32)
    x2 = r1.astype(BF16)
    x3 = (r1 - x2.astype(F32)).astype(BF16)
    return x1, x2, x3


def _rms(x, g):
    return x * lax.rsqrt(jnp.mean(x * x, axis=-1, keepdims=True) + EPS) * g


def _norm_proj_kernel(*refs, has_aux):
    if has_aux:
        x_ref, g_ref, w_ref, waux_ref, o_ref, oaux_ref = refs
    else:
        x_ref, g_ref, w_ref, o_ref = refs
    h = _rms(x_ref[...], g_ref[...]).astype(BF16)
    for k in range(w_ref.shape[1] // PROJ_COLS):
        sl = slice(k * PROJ_COLS, (k + 1) * PROJ_COLS)
        o_ref[:, sl] = _dot(h, w_ref[:, sl]).astype(o_ref.dtype)
    if has_aux:
        oaux_ref[...] = _dot(h, waux_ref[...])


def _norm_proj(x, g, w, w_aux, *, name):
    m, d = x.shape
    n = w.shape[1]
    tm = PROJ_ROWS
    has_aux = w_aux is not None
    resident = dict(pipeline_mode=pl.Buffered(1))
    in_specs = [pl.BlockSpec((tm, d), lambda i: (i, 0)),
                pl.BlockSpec((1, d), lambda i: (0, 0)),
                pl.BlockSpec((d, n), lambda i: (0, 0), **resident)]
    out_specs = [pl.BlockSpec((tm, n), lambda i: (i, 0))]
    out_shape = [jax.ShapeDtypeStruct((m, n), ACT_DTYPE)]
    args = [x, g.reshape(1, d), w]
    if has_aux:
        in_specs.append(pl.BlockSpec((d, LANES), lambda i: (0, 0), **resident))
        out_specs.append(pl.BlockSpec((tm, LANES), lambda i: (i, 0)))
        out_shape.append(jax.ShapeDtypeStruct((m, LANES), F32))
        args.append(w_aux)
    return pl.pallas_call(
        functools.partial(_norm_proj_kernel, has_aux=has_aux),
        grid=(m // tm,),
        in_specs=in_specs, out_specs=out_specs, out_shape=out_shape,
        compiler_params=_cparams("parallel"),
        name=name,
    )(*args)


def _even_proj_kernel(x_ref, g_ref, w_ref, wdt_ref, cw_ref, cb_ref, o_ref, odt_ref, tail_ref, *, tiles_per_seq):
    tm = x_ref.shape[0]

    @pl.when(pl.program_id(0) % tiles_per_seq == 0)
    def _():
        tail_ref[...] = jnp.zeros(tail_ref.shape, F32)

    h = _rms(x_ref[...], g_ref[...]).astype(BF16)
    c = 2 * LANES
    for k in range(SSD_XBC // c):
        sl = slice(k * c, (k + 1) * c)
        r = _dot(h, w_ref[:, sl])
        ext = jnp.concatenate([tail_ref[:, sl], r], axis=0)
        acc = cb_ref[:, sl] + cw_ref[SSD_CONV - 1:SSD_CONV, sl] * r
        for d in range(1, SSD_CONV):
            acc = acc + cw_ref[SSD_CONV - 1 - d:SSD_CONV - d, sl] * ext[SUBLANES - d:SUBLANES - d + tm]
        tail_ref[:, sl] = r[tm - SUBLANES:]
        o_ref[:, sl] = (acc * jax.nn.sigmoid(acc)).astype(o_ref.dtype)
    for k in range(SSD_XBC // PROJ_COLS, w_ref.shape[1] // PROJ_COLS):
        sl = slice(k * PROJ_COLS, (k + 1) * PROJ_COLS)
        o_ref[:, sl] = _dot(h, w_ref[:, sl]).astype(o_ref.dtype)
    odt_ref[...] = _dot(h, wdt_ref[...])


def _even_proj(x, g, w, w_dt, conv_w, conv_b, *, seq):
    m, d = x.shape
    n = w.shape[1]
    tm = PROJ_ROWS
    resident = dict(pipeline_mode=pl.Buffered(1))
    return pl.pallas_call(
        functools.partial(_even_proj_kernel, tiles_per_seq=seq // tm),
        grid=(m // tm,),
        in_specs=[pl.BlockSpec((tm, d), lambda i: (i, 0)),
                  pl.BlockSpec((1, d), lambda i: (0, 0)),
                  pl.BlockSpec((d, n), lambda i: (0, 0), **resident),
                  pl.BlockSpec((d, LANES), lambda i: (0, 0), **resident),
                  pl.BlockSpec((SSD_CONV, SSD_XBC), lambda i: (0, 0)),
                  pl.BlockSpec((1, SSD_XBC), lambda i: (0, 0))],
        out_specs=[pl.BlockSpec((tm, n), lambda i: (i, 0)),
                   pl.BlockSpec((tm, LANES), lambda i: (i, 0))],
        out_shape=[jax.ShapeDtypeStruct((m, n), ACT_DTYPE),
                   jax.ShapeDtypeStruct((m, LANES), F32)],
        scratch_shapes=[pltpu.VMEM((SUBLANES, SSD_XBC), F32)],
        compiler_params=_cparams("arbitrary"),
        name="even_in_proj",
    )(x, g.reshape(1, d), w, w_dt, conv_w.astype(F32), conv_b.astype(F32).reshape(1, SSD_XBC))


def _ssd_kernel(xbc_ref, dt_ref, dtb_ref, alog_ref, dskip_ref, e_ref, o_ref, state_ref):
    L = SSD_CHUNK
    N = SSD_STATE
    gw = SSD_INNER // SSD_GROUPS

    @pl.when(pl.program_id(1) == 0)
    def _():
        state_ref[...] = jnp.zeros(state_ref.shape, F32)

    lane = lax.broadcasted_iota(jnp.int32, (L, LANES), 1)
    low = lane < SSD_HEAD_DIM
    row = lax.broadcasted_iota(jnp.int32, (L, L), 0)
    col = lax.broadcasted_iota(jnp.int32, (L, L), 1)
    tril = row >= col
    tri = tril.astype(BF16)

    def per_head_to_lanes(v):
        v1 = v.astype(BF16)
        v2 = (v - v1.astype(F32)).astype(BF16)
        return _dot(v1, e_ref[...]) + _dot(v2, e_ref[...])

    for ci in range(xbc_ref.shape[0] // L):
        rs = slice(ci * L, (ci + 1) * L)
        xbc = xbc_ref[rs, :].astype(F32)
        xs = xbc[:, :SSD_INNER]
        bm = xbc[:, SSD_INNER:SSD_INNER + SSD_GROUPS * N]
        cm = xbc[:, SSD_INNER + SSD_GROUPS * N:]

        dt = jax.nn.softplus(dt_ref[rs, :] + dtb_ref[...])
        a = jnp.where(lane < SSD_HEADS, dt * (-jnp.exp(alog_ref[...])), 0.0)
        a1, a2, a3 = _split3(a)
        a_cs = _dot(tri, a1) + _dot(tri, a2) + _dot(tri, a3)
        a_cs_t = a_cs.T
        ea = jnp.exp(a_cs)
        ds = jnp.exp(a_cs[L - 1:L, :] - a_cs)
        dt_x = per_head_to_lanes(dt)
        dtds_x = per_head_to_lanes(dt * ds)
        ea_x = per_head_to_lanes(ea)

        y_tiles = []
        for g in range(SSD_GROUPS):
            bg = bm[:, g * N:(g + 1) * N]
            cg = cm[:, g * N:(g + 1) * N].astype(BF16)
            bg_t = bg.T.astype(BF16)
            cb = _dot(cg, bg_t)
            prev = state_ref[g]
            y_off = _dot(cg, prev.astype(BF16))
            xd_tiles = []
            for jj in range(2):
                j = 2 * g + jj
                ps = slice(j * LANES, (j + 1) * LANES)
                xs_p = xs[:, ps]
                xdt_p = xs_p * dt_x[:, ps]
                scores = []
                for h in (2 * j, 2 * j + 1):
                    diff = a_cs[:, h:h + 1] - a_cs_t[h:h + 1, :]
                    lmat = jnp.exp(jnp.where(tril, diff, -jnp.inf))
                    scores.append((cb * lmat).astype(BF16))
                lhs = jnp.concatenate(scores, axis=1)
                rhs = jnp.concatenate([jnp.where(low, xdt_p, 0.0), jnp.where(low, 0.0, xdt_p)],
                                      axis=0).astype(BF16)
                y_diag = _dot(lhs, rhs)
                y_tiles.append(y_diag + y_off[:, jj * LANES:(jj + 1) * LANES] * ea_x[:, ps]
                               + dskip_ref[:, ps] * xs_p)
                xd_tiles.append((xs_p * dtds_x[:, ps]).astype(BF16))
            xd = jnp.concatenate(xd_tiles, axis=1)
            decay = ea_x[L - 1:L, g * gw:(g + 1) * gw]
            state_ref[g] = prev * decay + _dot(bg_t, xd)

        o_ref[rs, :] = jnp.concatenate(y_tiles, axis=1).astype(o_ref.dtype)


def _ssd_mixer(u_wide, dt_raw, dt_bias, a_log, d_skip, batch, seq):
    L = SSD_CHUNK
    T = SSD_STEP_CHUNKS * L
    ns = seq // T
    m = batch * seq

    def pad_heads(v):
        return jnp.pad(v.astype(F32), (0, LANES - SSD_HEADS)).reshape(1, LANES)

    dskip = jnp.repeat(d_skip.astype(F32), SSD_HEAD_DIM).reshape(1, SSD_INNER)
    e_np = np.zeros((LANES, SSD_INNER), np.float32)
    for hh in range(SSD_HEADS):
        e_np[hh, hh * SSD_HEAD_DIM:(hh + 1) * SSD_HEAD_DIM] = 1.0
    head_to_lanes = jnp.asarray(e_np, dtype=BF16)
    return pl.pallas_call(
        _ssd_kernel,
        grid=(batch, ns),
        in_specs=[
            pl.BlockSpec((T, SSD_XBC), lambda b, c: (b * ns + c, 0)),
            pl.BlockSpec((T, LANES), lambda b, c: (b * ns + c, 0)),
            pl.BlockSpec((1, LANES), lambda b, c: (0, 0)),
            pl.BlockSpec((1, LANES), lambda b, c: (0, 0)),
            pl.BlockSpec((1, SSD_INNER), lambda b, c: (0, 0)),
            pl.BlockSpec((LANES, SSD_INNER), lambda b, c: (0, 0)),
        ],
        out_specs=pl.BlockSpec((T, SSD_INNER), lambda b, c: (b * ns + c, 0)),
        out_shape=jax.ShapeDtypeStruct((m, SSD_INNER), ACT_DTYPE),
        scratch_shapes=[pltpu.VMEM((SSD_GROUPS, SSD_STATE, SSD_INNER // SSD_GROUPS), F32)],
        compiler_params=_cparams("parallel", "arbitrary"),
        name="ssd_mixer",
    )(u_wide, dt_raw, pad_heads(dt_bias), pad_heads(a_log), dskip, head_to_lanes)


def _conf_kernel(ag_ref, w_ref, b_ref, o_ref, sh_ref, y_ref):
    T = CONF_ROWS
    H = CONF_HALO
    strips = CONF_DIM // LANES

    @pl.when(pl.program_id(1) == 0)
    def _():
        sh_ref[...] = jnp.zeros(sh_ref.shape, F32)

    a = ag_ref[:, :CONF_DIM].astype(F32)
    gate = ag_ref[:, CONF_DIM:].astype(F32)
    h = a * jax.nn.sigmoid(gate)
    for s in range(SUBLANES):
        for c in range(strips):
            sh_ref[s, c, 0:H, :] = sh_ref[s, c, T:T + H, :]
            sh_ref[s, c, pl.ds(H - s, T), :] = h[:, c * LANES:(c + 1) * LANES]

    base = H - (CONF_KERNEL - 1)
    nt = 32
    groups = T // (nt * SUBLANES)

    def strip_group(gi, carry):
        c = gi // groups
        r0 = pl.multiple_of((gi % groups) * (nt * SUBLANES), nt * SUBLANES)
        tiles = [jnp.broadcast_to(b_ref[c], (SUBLANES, LANES))] * nt
        for j in range(CONF_KERNEL):
            q, s = divmod(base + j, SUBLANES)
            wj = jnp.broadcast_to(w_ref[c, j:j + 1, :], (SUBLANES, LANES))
            tiles = [t + wj * sh_ref[s, c, pl.ds(r0 + (q + i) * SUBLANES, SUBLANES), :]
                     for i, t in enumerate(tiles)]
        y_ref[c, pl.ds(r0, nt * SUBLANES), :] = jnp.concatenate(tiles, axis=0)
        return carry

    lax.fori_loop(0, strips * groups, strip_group, 0)

    for c in range(strips):
        o_ref[:, c * LANES:(c + 1) * LANES] = y_ref[c].astype(o_ref.dtype)


def _conf_mixer(u_wide, conv_w, conv_b, batch, seq):
    T = CONF_ROWS
    nt = seq // T
    m = batch * seq
    strips = CONF_DIM // LANES
    taps_pad = CONF_KERNEL + 1
    w_strips = jnp.pad(conv_w.astype(F32), ((0, 1), (0, 0))).reshape(taps_pad, strips, LANES).transpose(1, 0, 2)
    b_strips = conv_b.astype(F32).reshape(strips, 1, LANES)
    return pl.pallas_call(
        _conf_kernel,
        grid=(batch, nt),
        in_specs=[
            pl.BlockSpec((T, 2 * CONF_DIM), lambda b, c: (b * nt + c, 1)),
            pl.BlockSpec((strips, taps_pad, LANES), lambda b, c: (0, 0, 0)),
            pl.BlockSpec((strips, 1, LANES), lambda b, c: (0, 0, 0)),
        ],
        out_specs=pl.BlockSpec((T, CONF_DIM), lambda b, c: (b * nt + c, 0)),
        out_shape=jax.ShapeDtypeStruct((m, CONF_DIM), ACT_DTYPE),
        scratch_shapes=[pltpu.VMEM((SUBLANES, CONF_DIM // LANES, T + CONF_HALO, LANES), F32),
                        pltpu.VMEM((CONF_DIM // LANES, T, LANES), F32)],
        compiler_params=_cparams("parallel", "arbitrary"),
        name="conformer_mixer",
    )(u_wide, w_strips, b_strips)


def _finish_mixer(v_ref, aux_ref, gain_ref, bias_ref, kind):
    if kind is None:
        return v_ref[...]
    v = v_ref[...].astype(F32)
    width = 2 * LANES
    if kind == "ln_swish":
        mu = jnp.mean(v, axis=-1, keepdims=True)
        var = jnp.mean(jnp.square(v - mu), axis=-1, keepdims=True)
        hn = (v - mu) * lax.rsqrt(var + EPS) * gain_ref[...] + bias_ref[...]
        return (hn * jax.nn.sigmoid(hn)).astype(BF16)
    aux = aux_ref[...].astype(F32)
    gate = aux * jax.nn.sigmoid(aux)
    outs = []
    for gi in range(v.shape[1] // width):
        sl = slice(gi * width, (gi + 1) * width)
        if kind == "gate_rms":
            vg = v[:, sl] * gate[:, sl]
            outs.append(vg * lax.rsqrt(jnp.mean(vg * vg, axis=-1, keepdims=True) + EPS) * gain_ref[:, sl])
        else:
            vg = v[:, sl]
            mu = jnp.mean(vg, axis=-1, keepdims=True)
            var = jnp.mean(jnp.square(vg - mu), axis=-1, keepdims=True)
            on = (vg - mu) * lax.rsqrt(var + EPS) * gain_ref[:, sl] + bias_ref[:, sl]
            outs.append(gate[:, sl] * on)
    return jnp.concatenate(outs, axis=1).astype(BF16)


def _mix_ffn_kernel(*refs, a_kind, b_kind, final_norm):
    refs = list(refs)
    x_ref, a_ref, b_ref = refs[:3]
    del refs[:3]
    a_aux = refs.pop(0) if a_kind in ("gate_rms", "gn_gate") else None
    b_aux = refs.pop(0) if b_kind in ("gate_rms", "gn_gate") else None
    ag_ref, ab_ref, bg_ref, bb_ref, wa_ref, wb_ref, g_ref, wg_ref, wu_ref, wd_ref, fg_ref, o_ref, act_ref = refs
    a = _finish_mixer(a_ref, a_aux, ag_ref, ab_ref, a_kind)
    b = _finish_mixer(b_ref, b_aux, bg_ref, bb_ref, b_kind)
    x = x_ref[...] + _dot(a, wa_ref[...]) + _dot(b, wb_ref[...])
    h = _rms(x, g_ref[...]).astype(BF16)
    for k in range(wg_ref.shape[1] // FFN_COLS):
        sl = slice(k * FFN_COLS, (k + 1) * FFN_COLS)
        gate = _dot(h, wg_ref[:, sl])
        up = _dot(h, wu_ref[:, sl])
        act_ref[:, sl] = (gate * jax.nn.sigmoid(gate) * up).astype(BF16)
    y = x + _dot(act_ref[...], wd_ref[...])
    if final_norm:
        y = _rms(y, fg_ref[...])
    o_ref[...] = y


def _mix_ffn(x, a, a_tail, b, b_tail, w_out, g, w_gate, w_up, w_down, final_g, name):
    m, d = x.shape
    ka, kb = a.shape[1], b.shape[1]
    f = w_gate.shape[1]
    tm = FFN_ROWS
    final_norm = final_g is not None
    fg = (final_g if final_norm else jnp.ones((d,), F32)).astype(F32).reshape(1, d)
    resident = dict(pipeline_mode=pl.Buffered(1))
    row_block = lambda k: pl.BlockSpec((tm, k), lambda i: (i, 0))
    vec_block = lambda k: pl.BlockSpec((1, k), lambda i: (0, 0))
    args = [x, a, b]
    in_specs = [row_block(d), row_block(ka), row_block(kb)]
    kinds, vecs = [], []
    for tail, k in ((a_tail, ka), (b_tail, kb)):
        kind, aux, gain, bias = tail if tail is not None else (None, None, None, None)
        kinds.append(kind)
        if aux is not None:
            arr, blk = aux
            args.append(arr)
            in_specs.append(pl.BlockSpec((tm, k), lambda i, blk=blk: (i, blk)))
        gain = jnp.ones((k,), F32) if gain is None else gain
        bias = jnp.zeros((k,), F32) if bias is None else bias
        vecs += [gain.astype(F32).reshape(1, k), bias.astype(F32).reshape(1, k)]
    args += vecs + [w_out[:ka].astype(BF16), w_out[ka:].astype(BF16), g.astype(F32).reshape(1, d),
                    w_gate.astype(BF16), w_up.astype(BF16), w_down.astype(BF16), fg]
    in_specs += [vec_block(ka), vec_block(ka), vec_block(kb), vec_block(kb),
                 pl.BlockSpec((ka, d), lambda i: (0, 0), **resident),
                 pl.BlockSpec((kb, d), lambda i: (0, 0), **resident),
                 vec_block(d),
                 pl.BlockSpec((d, f), lambda i: (0, 0), **resident),
                 pl.BlockSpec((d, f), lambda i: (0, 0), **resident),
                 pl.BlockSpec((f, d), lambda i: (0, 0), **resident),
                 vec_block(d)]
    return pl.pallas_call(
        functools.partial(_mix_ffn_kernel, a_kind=kinds[0], b_kind=kinds[1], final_norm=final_norm),
        grid=(m // tm,),
        in_specs=in_specs,
        out_specs=pl.BlockSpec((tm, d), lambda i: (i, 0)),
        out_shape=jax.ShapeDtypeStruct((m, d), F32),
        scratch_shapes=[pltpu.VMEM((tm, f), BF16)],
        compiler_params=_cparams("parallel"),
        name=name,
    )(*args)


def _s5_kernel(u_ref, bre_ref, bim_ref, cre_ref, cim_ref, lre_ref, lim_ref, d_ref,
               gw_ref, gb_ref, o_ref, sre_ref, sim_ref, st_ref, rl_ref):
    nb, ts, dim = u_ref.shape
    rows = nb * ts
    half_in = dim // 2

    @pl.when(pl.program_id(0) == 0)
    def _():
        st_ref[...] = jnp.zeros(st_ref.shape, F32)

    for b in range(nb):
        ub = u_ref[b].astype(F32)
        for k in range(dim // LANES):
            rl_ref[k, pl.ds(b, ts, stride=nb), :] = ub[:, k * LANES:(k + 1) * LANES]
    u_tm = jnp.concatenate([rl_ref[k] for k in range(dim // LANES)], axis=1)
    u16 = u_tm.astype(BF16)

    for hf in range(2):
        uh = u16[:, hf * half_in:(hf + 1) * half_in]
        sre_ref[:, hf * S5_HALF:(hf + 1) * S5_HALF] = _dot(uh, bre_ref[hf])
        sim_ref[:, hf * S5_HALF:(hf + 1) * S5_HALF] = _dot(uh, bim_ref[hf])

    for hf in range(2):
        sl = slice(hf * S5_HALF, (hf + 1) * S5_HALF)
        lr = lre_ref[:, sl]
        li = lim_ref[:, sl]

        def step(t, carry, sl=sl, lr=lr, li=li):
            sr, si = carry
            r0 = pl.multiple_of(t * nb, nb)
            nr = lr * sr - li * si + sre_ref[pl.ds(r0, nb), sl]
            ni = lr * si + li * sr + sim_ref[pl.ds(r0, nb), sl]
            sre_ref[pl.ds(r0, nb), sl] = nr
            sim_ref[pl.ds(r0, nb), sl] = ni
            return nr, ni

        sr, si = lax.fori_loop(0, ts, step, (st_ref[0, :, sl], st_ref[1, :, sl]), unroll=True)
        st_ref[0, :, sl] = sr
        st_ref[1, :, sl] = si

    ys = []
    for hf in range(2):
        sl = slice(hf * S5_HALF, (hf + 1) * S5_HALF)
        ys.append(_dot(sre_ref[:, sl].astype(BF16), cre_ref[hf]) + _dot(sim_ref[:, sl].astype(BF16), cim_ref[hf]))
    y = jnp.concatenate(ys, axis=1) + d_ref[...] * u_tm
    y = jax.nn.gelu(y)
    y = y * jax.nn.sigmoid(_dot(y.astype(BF16), gw_ref[...]) + gb_ref[...])
    for k in range(dim // LANES):
        rl_ref[k] = y[:, k * LANES:(k + 1) * LANES]
    for b in range(nb):
        for k in range(dim // LANES):
            o_ref[b, :, k * LANES:(k + 1) * LANES] = rl_ref[k, pl.ds(b, ts, stride=nb), :].astype(o_ref.dtype)


def _s5_mixer(u_wide3, lam_re, lam_im, b_re, b_im, c_re, c_im, log_dt, d_skip, glu_w, glu_b, batch, seq):
    ts = S5_STEPS
    rows = batch * ts
    G, P, C = S5_GROUPS, S5_STATE, S5_GROUP
    hg = G // 2
    lam = lax.complex(lam_re.astype(F32), lam_im.astype(F32))
    dt = jnp.exp(log_dt.astype(F32))[:, None]
    lam_bar = jnp.exp(lam * dt)
    b_bar = ((lam_bar - 1.0) / lam)[..., None] * lax.complex(b_re.astype(F32), b_im.astype(F32))
    eye = jnp.eye(hg, dtype=F32)

    def blockdiag_in(v):
        blk = v.transpose(0, 2, 1).reshape(2, hg, C, P)
        return jnp.einsum('hgcp,gk->hgckp', blk, eye).reshape(2, hg * C, hg * P).astype(BF16)

    def blockdiag_out(v):
        blk = v.transpose(0, 2, 1).reshape(2, hg, P, C)
        return jnp.einsum('hgpc,gk->hgpkc', blk, eye).reshape(2, hg * P, hg * C).astype(BF16)

    bre = blockdiag_in(jnp.real(b_bar))
    bim = blockdiag_in(jnp.imag(b_bar))
    cre = blockdiag_out(c_re.astype(F32))
    cim = blockdiag_out(-c_im.astype(F32))
    lre = jnp.broadcast_to(jnp.real(lam_bar).reshape(1, G * P), (batch, G * P))
    lim = jnp.broadcast_to(jnp.imag(lam_bar).reshape(1, G * P), (batch, G * P))
    ublk = u_wide3.shape[2] // S5_DIM - 1
    const2 = lambda i: (0, 0)
    const3 = lambda i: (0, 0, 0)
    return pl.pallas_call(
        _s5_kernel,
        grid=(seq // ts,),
        in_specs=[
            pl.BlockSpec((batch, ts, S5_DIM), lambda i: (0, i, ublk)),
            pl.BlockSpec((2, hg * C, hg * P), const3),
            pl.BlockSpec((2, hg * C, hg * P), const3),
            pl.BlockSpec((2, hg * P, hg * C), const3),
            pl.BlockSpec((2, hg * P, hg * C), const3),
            pl.BlockSpec((batch, G * P), const2),
            pl.BlockSpec((batch, G * P), const2),
            pl.BlockSpec((1, S5_DIM), const2),
            pl.BlockSpec((S5_DIM, S5_DIM), const2),
            pl.BlockSpec((1, S5_DIM), const2),
        ],
        out_specs=pl.BlockSpec((batch, ts, S5_DIM), lambda i: (0, i, 0)),
        out_shape=jax.ShapeDtypeStruct((batch, seq, S5_DIM), ACT_DTYPE),
        scratch_shapes=[pltpu.VMEM((rows, S5_LANES), F32), pltpu.VMEM((rows, S5_LANES), F32),
                        pltpu.VMEM((2, batch, S5_LANES), F32),
                        pltpu.VMEM((S5_DIM // LANES, rows, LANES), F32)],
        compiler_params=_cparams("arbitrary"),
        name="s5_mixer",
    )(u_wide3, bre, bim, cre, cim, lre, lim, d_skip.astype(F32).reshape(1, S5_DIM),
      glu_w.astype(BF16), glu_b.astype(F32).reshape(1, S5_DIM))


def _ret_kernel(q_ref, k_ref, v_ref, cos_ref, sin_ref, dmat_ref, qdec_ref, kdec_ref, cg_ref, o_ref, state_ref):
    L = RET_CHUNK
    dk, dv = RET_KEY_DIM, RET_VAL_DIM
    half = dk // 2

    @pl.when(pl.program_id(1) == 0)
    def _():
        state_ref[...] = jnp.zeros(state_ref.shape, F32)

    def rot(t, cos, sin):
        t1, t2 = t[:, :half], t[:, half:]
        return jnp.concatenate([t1 * cos - t2 * sin, t1 * sin + t2 * cos], axis=1)

    for ci in range(q_ref.shape[0] // L):
        rs = slice(ci * L, (ci + 1) * L)
        cos = cos_ref[rs, :]
        sin = sin_ref[rs, :]
        for h in range(RET_HEADS):
            q = rot(q_ref[rs, h * dk:(h + 1) * dk].astype(F32), cos, sin)
            k = rot(k_ref[rs, h * dk:(h + 1) * dk].astype(F32), cos, sin) * (dk ** -0.5)
            v = v_ref[rs, h * dv:(h + 1) * dv]
            q16 = q.astype(BF16)
            k_t = k.T
            scores = _dot(q16, k_t.astype(BF16)) * dmat_ref[h]
            prev = state_ref[h]
            o = _dot(scores.astype(BF16), v) + _dot(q16, prev.astype(BF16)) * qdec_ref[:, h:h + 1]
            kv = _dot((k_t * kdec_ref[h:h + 1, :]).astype(BF16), v)
            state_ref[h] = prev * cg_ref[h:h + 1, :] + kv
            o_ref[rs, h * dv:(h + 1) * dv] = o.astype(o_ref.dtype)


def _ret_mixer(u_wide, batch, seq):
    L = RET_CHUNK
    m = batch * seq
    inv_freq = ROPE_BASE ** (-jnp.arange(0, RET_KEY_DIM, 2, dtype=F32) / RET_KEY_DIM)
    ang = jnp.arange(seq, dtype=F32)[:, None] * inv_freq[None, :]
    cos, sin = jnp.cos(ang), jnp.sin(ang)
    gamma = 1.0 - jnp.exp(jnp.linspace(math.log(1.0 / 32), math.log(1.0 / 512), RET_HEADS, dtype=F32))
    log_g = jnp.log(gamma)
    idx = jnp.arange(L, dtype=F32)
    diff = idx[:, None] - idx[None, :]
    dmat = jnp.where(diff >= 0, jnp.exp(log_g[:, None, None] * jnp.maximum(diff, 0.0)), 0.0)
    kdec = jnp.exp(log_g[:, None] * (L - 1.0 - idx)[None, :])
    kdec = jnp.pad(kdec, ((0, SUBLANES - RET_HEADS), (0, 0)))
    qdec = jnp.exp(log_g[None, :] * (idx + 1.0)[:, None])
    qdec = jnp.pad(qdec, ((0, 0), (0, LANES - RET_HEADS)))
    cg = jnp.broadcast_to(jnp.exp(log_g * L)[:, None], (RET_HEADS, RET_VAL_DIM))
    cg = jnp.pad(cg, ((0, SUBLANES - RET_HEADS), (0, 0)))
    T = RET_STEP_CHUNKS * L
    ns = seq // T
    row = lambda b, c: b * ns + c
    return pl.pallas_call(
        _ret_kernel,
        grid=(batch, ns),
        in_specs=[
            pl.BlockSpec((T, RET_QK), lambda b, c: (row(b, c), 0)),
            pl.BlockSpec((T, RET_QK), lambda b, c: (row(b, c), 1)),
            pl.BlockSpec((T, RET_V), lambda b, c: (row(b, c), 2)),
            pl.BlockSpec((T, RET_KEY_DIM // 2), lambda b, c: (c, 0)),
            pl.BlockSpec((T, RET_KEY_DIM // 2), lambda b, c: (c, 0)),
            pl.BlockSpec((RET_HEADS, L, L), lambda b, c: (0, 0, 0)),
            pl.BlockSpec((L, LANES), lambda b, c: (0, 0)),
            pl.BlockSpec((SUBLANES, L), lambda b, c: (0, 0)),
            pl.BlockSpec((SUBLANES, RET_VAL_DIM), lambda b, c: (0, 0)),
        ],
        out_specs=pl.BlockSpec((T, RET_V), lambda b, c: (row(b, c), 0)),
        out_shape=jax.ShapeDtypeStruct((m, RET_V), ACT_DTYPE),
        scratch_shapes=[pltpu.VMEM((RET_HEADS, RET_KEY_DIM, RET_VAL_DIM), F32)],
        compiler_params=_cparams("parallel", "arbitrary"),
        name="retention_mixer",
    )(u_wide, u_wide, u_wide, cos, sin, dmat, qdec, kdec, cg)


def kernel(x, ev_norm, ev_w_in, ev_conv_w, ev_conv_b, ev_dt_bias, ev_a_log, ev_d, ev_ssd_norm, ev_cf_conv_w, ev_cf_conv_b, ev_cf_ln_g, ev_cf_ln_b, ev_w_out, od_norm, od_w_in, od_lam_re, od_lam_im, od_b_re, od_b_im, od_c_re, od_c_im, od_log_dt, od_s5_d, od_glu_w, od_glu_b, od_gn_g, od_gn_b, od_w_out, ffn_norm, ffn_w_gate, ffn_w_up, ffn_w_down, final_norm):
    batch, seq, d = x.shape
    m = batch * seq
    xf = x.reshape(m, d)

    w = ev_w_in[0]
    o1 = SSD_INNER
    o2 = o1 + SSD_XBC
    o3 = o2 + SSD_HEADS
    w_main = jnp.concatenate([w[:, o1:o2], w[:, o3:], w[:, :o1]], axis=1).astype(BF16)
    w_dt = jnp.pad(w[:, o2:o3], ((0, 0), (0, LANES - SSD_HEADS))).astype(BF16)
    u_wide, dt_raw = _even_proj(xf, ev_norm[0], w_main, w_dt, ev_conv_w[0], ev_conv_b[0], seq=seq)
    ya = _ssd_mixer(u_wide, dt_raw, ev_dt_bias[0], ev_a_log[0], ev_d[0], batch, seq)
    yb = _conf_mixer(u_wide, ev_cf_conv_w[0], ev_cf_conv_b[0], batch, seq)
    z_block = (SSD_XBC + 2 * CONF_DIM) // SSD_INNER
    xf = _mix_ffn(xf, ya, ("gate_rms", (u_wide, z_block), ev_ssd_norm[0], None),
                  yb, ("ln_swish", None, ev_cf_ln_g[0], ev_cf_ln_b[0]),
                  ev_w_out[0], ffn_norm[0], ffn_w_gate[0], ffn_w_up[0], ffn_w_down[0], None, "even_out_ffn")

    w = od_w_in[0]
    w_main = jnp.concatenate([w[:, S5_DIM:], w[:, :S5_DIM]], axis=1).astype(BF16)
    (u_wide,) = _norm_proj(xf, od_norm[0], w_main, None, name="odd_in_proj")
    yc = _s5_mixer(u_wide.reshape(batch, seq, -1), od_lam_re[0], od_lam_im[0], od_b_re[0], od_b_im[0],
                   od_c_re[0], od_c_im[0], od_log_dt[0], od_s5_d[0], od_glu_w[0], od_glu_b[0], batch, seq)
    yd = _ret_mixer(u_wide, batch, seq)
    gate_block = (2 * RET_QK + RET_V) // RET_V
    xf = _mix_ffn(xf, yc.reshape(m, S5_DIM), None,
                  yd, ("gn_gate", (u_wide, gate_block), od_gn_g[0], od_gn_b[0]),
                  od_w_out[0], ffn_norm[1], ffn_w_gate[1], ffn_w_up[1], ffn_w_down[1], final_norm, "odd_out_ffn")
    return xf.reshape(batch, seq, d)
```
